```python
import jax, jax.numpy as jnp
from jax import lax
import numpy as np

D_MODEL = 2048
BATCH = 2
SEQ = 8192
DEPTH = 4

GRID_W = 64
CTX_LEN = 256
HEAD_DIM = 128
N_Q_HEADS = 8
N_KV_HEADS = 2
Q_PER_KV = N_Q_HEADS // N_KV_HEADS
ATTN_W = N_Q_HEADS * HEAD_DIM
KV_W = N_KV_HEADS * HEAD_DIM
WINDOW = 128
BLOCK = 128
ROPE_BASE = 10000.0
ROT_HALF = HEAD_DIM // 2
CONV_W = D_MODEL // 2
CONV_K = 3
MIX_W = ATTN_W + CONV_W
OFF_C = CONV_W
OFF_X = 2 * CONV_W
OFF_Q = 3 * CONV_W
OFF_K = OFF_Q + ATTN_W
OFF_V = OFF_K + KV_W
IN_COLS = OFF_V + KV_W
PEER_HEADS = 8
N_KEYS = 128
N_EXPERTS = N_KEYS * N_KEYS
PEER_TOPK = 16
PEER_DKEY = 256
PEER_HALF = PEER_DKEY // 2
PEER_CHUNK = 128
DEEPNORM_ALPHA = (2.0 * DEPTH) ** 0.25
DEEPNORM_BETA = (8.0 * DEPTH) ** -0.25
LN_EPS = 1e-6

kernel_name = "hymba_conv_swa_peer_dit"


def layer_norm(x):
    xf = x.astype(jnp.float32)
    mu = jnp.mean(xf, axis=-1, keepdims=True)
    var = jnp.mean(jnp.square(xf - mu), axis=-1, keepdims=True)
    return ((xf - mu) * lax.rsqrt(var + LN_EPS)).astype(x.dtype)


def post_norm(x, g, b):
    return layer_norm(x) * g + b


def modulate(x, shift, scale):
    return layer_norm(x) * (1 + scale) + shift


def group_rms(y, gain):
    b_, t, w = y.shape
    yf = y.astype(jnp.float32).reshape(b_, t, w // HEAD_DIM, HEAD_DIM)
    yf = yf * lax.rsqrt(jnp.mean(jnp.square(yf), axis=-1, keepdims=True) + LN_EPS)
    return yf.reshape(b_, t, w).astype(y.dtype) * gain


def axial_rope_tables(n_tokens):
    rows = n_tokens // GRID_W
    row = jnp.broadcast_to(jnp.arange(rows)[:, None], (rows, GRID_W)).reshape(-1)
    col = jnp.broadcast_to(jnp.arange(GRID_W)[None, :], (rows, GRID_W)).reshape(-1)
    inv = ROPE_BASE ** (-jnp.arange(0, ROT_HALF, 2, dtype=jnp.float32) / ROT_HALF)
    ar = row.astype(jnp.float32)[:, None] * inv
    ac = col.astype(jnp.float32)[:, None] * inv
    ang = jnp.concatenate([ar, ar, ac, ac], axis=-1)
    return jnp.cos(ang), jnp.sin(ang)


def apply_rope(x, cos, sin):
    x1, x2, x3, x4 = jnp.split(x, 4, axis=-1)
    rot = jnp.concatenate([-x2, x1, -x4, x3], axis=-1)
    return (x * cos[:, None, :] + rot * sin[:, None, :]).astype(x.dtype)


def short_conv(u, w):
    up = jnp.pad(u, ((0, 0), (1, 1), (0, 0)))
    return up[:, :-2] * w[0] + up[:, 1:-1] * w[1] + up[:, 2:] * w[2]


def sink_softmax(logits, sink):
    full = jnp.concatenate([logits, jnp.broadcast_to(sink, logits.shape[:-1] + (1,))], axis=-1)
    return jax.nn.softmax(full, axis=-1)[..., :-1]


def banded(t):
    b_, s, g, d = t.shape
    nb = s // BLOCK
    tp = jnp.pad(t, ((0, 0), (BLOCK, BLOCK), (0, 0), (0, 0))).reshape(b_, nb + 2, BLOCK, g, d)
    return jnp.concatenate([tp[:, :-2], tp[:, 1:-1], tp[:, 2:]], axis=2)


def latent_attention(q, k, v, kc, vc, sink):
    b_, s = q.shape[0], q.shape[1]
    nb = s // BLOCK
    scale = HEAD_DIM ** -0.5
    qb = q.reshape(b_, nb, BLOCK, N_KV_HEADS, Q_PER_KV, HEAD_DIM)
    kw, vw = banded(k), banded(v)
    s_win = jnp.einsum('bnqgrd,bnkgd->bngrqk', qb, kw).astype(jnp.float32) * scale
    s_ctx = jnp.einsum('bnqgrd,blgd->bngrql', qb, kc).astype(jnp.float32) * scale
    qpos = jnp.arange(nb)[:, None] * BLOCK + jnp.arange(BLOCK)[None, :]
    kpos = (jnp.arange(nb)[:, None] - 1) * BLOCK + jnp.arange(3 * BLOCK)[None, :]
    valid = ((jnp.abs(qpos[:, :, None] - kpos[:, None, :]) <= WINDOW)
             & (kpos[:, None, :] >= 0) & (kpos[:, None, :] < s))
    s_win = jnp.where(valid[None, :, None, None], s_win, -jnp.inf)
    sink_b = sink.astype(jnp.float32).reshape(1, 1, N_KV_HEADS, Q_PER_KV, 1, 1)
    p = sink_softmax(jnp.concatenate([s_win, s_ctx], axis=-1), sink_b).astype(v.dtype)
    out = (jnp.einsum('bngrqk,bnkgd->bnqgrd', p[..., :3 * BLOCK], vw)
           + jnp.einsum('bngrql,blgd->bnqgrd', p[..., 3 * BLOCK:], vc))
    return out.reshape(b_, s, ATTN_W)


def context_attention(qc, kc, vc, sink):
    b_, l = qc.shape[0], qc.shape[1]
    qg = qc.reshape(b_, l, N_KV_HEADS, Q_PER_KV, HEAD_DIM)
    s = jnp.einsum('blgrd,bmgd->bgrlm', qg, kc).astype(jnp.float32) * (HEAD_DIM ** -0.5)
    p = sink_softmax(s, sink.astype(jnp.float32).reshape(1, N_KV_HEADS, Q_PER_KV, 1, 1)).astype(vc.dtype)
    return jnp.einsum('bgrlm,bmgd->blgrd', p, vc).reshape(b_, l, ATTN_W)


def split_proj(p):
    return jnp.split(p, [OFF_C, OFF_X, OFF_Q, OFF_K, OFF_V], axis=-1)


def merge_groups(y_attn, y_conv, g_attn, g_conv, w_out):
    return jnp.concatenate([group_rms(y_attn, g_attn), group_rms(y_conv, g_conv)], axis=-1) @ w_out


def mixer(h, hc, w_in, conv_w, sink, g_attn, g_conv, w_out, cos, sin, ctx_out):
    b_, s, _ = h.shape
    l = hc.shape[1]
    gb, gc, xt, q, k, v = split_proj(h @ w_in)
    y_conv = gb * short_conv(gc * xt, conv_w)
    q = apply_rope(q.reshape(b_, s, N_Q_HEADS, HEAD_DIM), cos, sin)
    k = apply_rope(k.reshape(b_, s, N_KV_HEADS, HEAD_DIM), cos, sin)
    v = v.reshape(b_, s, N_KV_HEADS, HEAD_DIM)
    if ctx_out:
        gbc, gcc, xtc, qc, kc, vc = split_proj(hc @ w_in)
    else:
        kc, vc = jnp.split(hc @ w_in[:, OFF_K:], 2, axis=-1)
    kc = kc.reshape(b_, l, N_KV_HEADS, HEAD_DIM)
    vc = vc.reshape(b_, l, N_KV_HEADS, HEAD_DIM)
    y_attn = latent_attention(q, k, v, kc, vc, sink)
    y = merge_groups(y_attn, y_conv, g_attn, g_conv, w_out)
    if not ctx_out:
        return y, None
    yc_conv = gbc * short_conv(gcc * xtc, conv_w)
    yc_attn = context_attention(qc.reshape(b_, l, N_Q_HEADS, HEAD_DIM), kc, vc, sink)
    yc = merge_groups(yc_attn, yc_conv, g_attn, g_conv, w_out)
    return y, yc


def peer(h, w_q, sub_keys, u_tab, v_tab):
    b_, t, d = h.shape
    q = (h @ w_q).reshape(b_, t, PEER_HEADS, 2, PEER_HALF)
    s = jnp.einsum('bthpd,hpkd->bthpk', q, sub_keys).astype(jnp.float32)
    s1, i1 = lax.top_k(s[..., 0, :], PEER_TOPK)
    s2, i2 = lax.top_k(s[..., 1, :], PEER_TOPK)
    cand_s = (s1[..., :, None] + s2[..., None, :]).reshape(b_, t, PEER_HEADS, PEER_TOPK * PEER_TOPK)
    cand_i = (i1[..., :, None] * N_KEYS + i2[..., None, :]).reshape(b_, t, PEER_HEADS, PEER_TOPK * PEER_TOPK)
    top_s, top_pos = lax.top_k(cand_s, PEER_TOPK)
    idx = jnp.take_along_axis(cand_i, top_pos, axis=-1)
    gate = jax.nn.softmax(top_s, axis=-1).astype(h.dtype)
    n = (b_ * t) // PEER_CHUNK

    def experts(args):
        hb, ib, gb = args
        a = jax.nn.gelu(jnp.einsum('chkd,cd->chk', u_tab[ib], hb), approximate=False)
        return jnp.einsum('chk,chkd->cd', gb * a, v_tab[ib])

    out = lax.map(experts, (h.reshape(n, PEER_CHUNK, d),
                            idx.reshape(n, PEER_CHUNK, PEER_HEADS, PEER_TOPK),
                            gate.reshape(n, PEER_CHUNK, PEER_HEADS, PEER_TOPK)))
    return out.reshape(b_, t, d)


def setup_inputs(seed: int = 0) -> dict:
    key = jax.random.key(seed)
    ks = jax.random.split(key, 22)
    f32 = jnp.float32

    def nrm(k, shape, std):
        return jax.random.normal(k, shape, f32) * std

    return {
        "x": nrm(ks[0], (BATCH, SEQ, D_MODEL), 1.0),
        "c": nrm(ks[1], (BATCH, D_MODEL), 1.0),
        "ctx": nrm(ks[2], (BATCH, CTX_LEN, D_MODEL), 1.0),
        "c_ctx": nrm(ks[3], (D_MODEL,), 1.0),
        "w_ada": nrm(ks[4], (DEPTH, D_MODEL, 6 * D_MODEL), D_MODEL ** -0.5),
        "b_ada": nrm(ks[5], (DEPTH, 6 * D_MODEL), 0.02),
        "w_in": nrm(ks[6], (DEPTH, D_MODEL, IN_COLS), D_MODEL ** -0.5),
        "conv_w": nrm(ks[7], (DEPTH, CONV_K, CONV_W), CONV_K ** -0.5),
        "attn_sink": nrm(ks[8], (DEPTH, N_Q_HEADS), 0.5),
        "g_attn": 1.0 + nrm(ks[9], (DEPTH, ATTN_W), 0.02),
        "g_conv": 1.0 + nrm(ks[10], (DEPTH, CONV_W), 0.02),
        "w_out": nrm(ks[11], (DEPTH, MIX_W, D_MODEL), DEEPNORM_BETA * MIX_W ** -0.5),
        "ln1_g": 1.0 + nrm(ks[12], (DEPTH, D_MODEL), 0.02),
        "ln1_b": nrm(ks[13], (DEPTH, D_MODEL), 0.02),
        "peer_wq": nrm(ks[14], (DEPTH, D_MODEL, PEER_HEADS * PEER_DKEY), D_MODEL ** -0.5),
        "peer_keys": nrm(ks[15], (DEPTH, PEER_HEADS, 2, N_KEYS, PEER_HALF), PEER_HALF ** -0.5),
        "peer_u": nrm(ks[16], (DEPTH, N_EXPERTS, D_MODEL), D_MODEL ** -0.5),
        "peer_v": nrm(ks[17], (DEPTH, N_EXPERTS, D_MODEL), DEEPNORM_BETA),
        "ln2_g": 1.0 + nrm(ks[18], (DEPTH, D_MODEL), 0.02),
        "ln2_b": nrm(ks[19], (DEPTH, D_MODEL), 0.02),
    }


def reference(x, c, ctx, c_ctx, w_ada, b_ada, w_in, conv_w, attn_sink, g_attn, g_conv, w_out,
              ln1_g, ln1_b, peer_wq, peer_keys, peer_u, peer_v, ln2_g, ln2_b):
    s = x.shape[1]
    cos, sin = axial_rope_tables(s)
    cond = jax.nn.silu(c)
    cond_ctx = jax.nn.silu(c_ctx)
    xc = ctx
    for layer in range(DEPTH):
        ctx_out = layer < DEPTH - 1
        mod = (cond @ w_ada[layer] + b_ada[layer])[:, None, :]
        mod_c = cond_ctx @ w_ada[layer] + b_ada[layer]
        sh1, sc1, ga1, sh2, sc2, ga2 = jnp.split(mod, 6, axis=-1)
        sh1c, sc1c, ga1c, sh2c, sc2c, ga2c = jnp.split(mod_c, 6, axis=-1)
        h = modulate(x, sh1, sc1)
        hc = modulate(xc, sh1c, sc1c)
        y, yc = mixer(h, hc, w_in[layer], conv_w[layer], attn_sink[layer], g_attn[layer],
                      g_conv[layer], w_out[layer], cos, sin, ctx_out)
        x = post_norm(DEEPNORM_ALPHA * x + ga1 * y, ln1_g[layer], ln1_b[layer])
        h = modulate(x, sh2, sc2)
        if ctx_out:
            xc = post_norm(DEEPNORM_ALPHA * xc + ga1c * yc, ln1_g[layer], ln1_b[layer])
            hc = modulate(xc, sh2c, sc2c)
            y_all = peer(jnp.concatenate([h, hc], axis=1), peer_wq[layer], peer_keys[layer],
                         peer_u[layer], peer_v[layer])
            y, yc = y_all[:, :s], y_all[:, s:]
            xc = post_norm(DEEPNORM_ALPHA * xc + ga2c * yc, ln2_g[layer], ln2_b[layer])
        else:
            y = peer(h, peer_wq[layer], peer_keys[layer], peer_u[layer], peer_v[layer])
        x = post_norm(DEEPNORM_ALPHA * x + ga2 * y, ln2_g[layer], ln2_b[layer])
    return x
```

```python
import functools
import math

import jax
import jax.numpy as jnp
from jax import lax
from jax.experimental import pallas as pl
from jax.experimental.pallas import tpu as pltpu

HEAD_DIM = 128
N_Q_HEADS = 8
N_KV_HEADS = 2
Q_PER_KV = N_Q_HEADS // N_KV_HEADS
GRID_W = 64
WINDOW = 128
BLOCK = 128
ROPE_BASE = 10000.0
ROT_HALF = HEAD_DIM // 2
CONV_K = 3
PEER_HEADS = 8
N_KEYS = 128
PEER_TOPK = 16
PEER_HALF = 128
LN_EPS = 1e-6
NEG_BIG = -1e30

LANES = 128
SUBLANES = 8
VMEM_LIMIT_BYTES = 56 * 1024 * 1024

TM_IN = 512
TN_IN = 1536
TM_OUT = 256
TM_PQ = 256
TK_TOPK = 64
TB_PEER = 8
TN_ADA = 512


def _layer_norm(x):
    mu = jnp.mean(x, axis=-1, keepdims=True)
    xc = x - mu
    var = jnp.mean(xc * xc, axis=-1, keepdims=True)
    return xc * lax.rsqrt(var + LN_EPS)


def _ada_kernel(condt_ref, w_ref, b_ref, o_ref, *, n_rows):
    ct = condt_ref[...]
    ct = ct * (1.0 / (1.0 + jnp.exp(-ct)))
    w = w_ref[0]
    rows = []
    for r in range(SUBLANES):
        if r < n_rows:
            rows.append(jnp.sum(w * ct[:, r:r + 1], axis=0, keepdims=True) + b_ref[0])
        else:
            rows.append(jnp.zeros_like(b_ref[0]))
    o_ref[0] = jnp.concatenate(rows, axis=0)


def _ada_call(condt, w_ada, b_ada, n_rows):
    depth, d, n6 = w_ada.shape
    return pl.pallas_call(
        functools.partial(_ada_kernel, n_rows=n_rows),
        grid=(depth, n6 // TN_ADA),
        in_specs=[
            pl.BlockSpec((d, SUBLANES), lambda l, j: (0, 0)),
            pl.BlockSpec((1, d, TN_ADA), lambda l, j: (l, 0, j)),
            pl.BlockSpec((1, 1, TN_ADA), lambda l, j: (l, 0, j)),
        ],
        out_specs=pl.BlockSpec((1, SUBLANES, TN_ADA), lambda l, j: (l, 0, j)),
        out_shape=jax.ShapeDtypeStruct((depth, SUBLANES, n6), jnp.float32),
        compiler_params=pltpu.CompilerParams(
            dimension_semantics=("arbitrary", "arbitrary"), vmem_limit_bytes=VMEM_LIMIT_BYTES),
        name="ada_mod",
    )(condt, w_ada, b_ada.reshape(depth, 1, n6))


def _in_proj_kernel(x_ref, mod_ref, w_ref, o_ref, h_scr):
    @pl.when(pl.program_id(1) == 0)
    def _():
        sh = mod_ref[0, 0:1, :]
        sc = mod_ref[0, 1:2, :]
        h_scr[...] = (_layer_norm(x_ref[...]) * (1.0 + sc) + sh).astype(jnp.bfloat16)

    o_ref[...] = jnp.dot(h_scr[...], w_ref[...], preferred_element_type=jnp.float32)


def _in_proj_call(x_all, mod_l, w_in_bf, mod_idx_fn):
    r, d = x_all.shape
    n = w_in_bf.shape[1]
    return pl.pallas_call(
        _in_proj_kernel,
        grid=(r // TM_IN, n // TN_IN),
        in_specs=[
            pl.BlockSpec((TM_IN, d), lambda i, j: (i, 0)),
            pl.BlockSpec((1, 6, d), lambda i, j: (mod_idx_fn(i * TM_IN), 0, 0)),
            pl.BlockSpec((d, TN_IN), lambda i, j: (0, j)),
        ],
        out_specs=pl.BlockSpec((TM_IN, TN_IN), lambda i, j: (i, j)),
        out_shape=jax.ShapeDtypeStruct((r, n), jnp.float32),
        scratch_shapes=[pltpu.VMEM((TM_IN, d), jnp.bfloat16)],
        compiler_params=pltpu.CompilerParams(
            dimension_semantics=("arbitrary", "arbitrary"), vmem_limit_bytes=VMEM_LIMIT_BYTES),
        name="in_proj",
    )(x_all, mod_l, w_in_bf)


def _rope(x, cos, sin_signed, first_or_third):
    rot = jnp.where(first_or_third, pltpu.roll(x, 3 * HEAD_DIM // 4, axis=1), pltpu.roll(x, HEAD_DIM // 4, axis=1))
    return x * cos + rot * sin_signed


def _attn_kernel(sink_ref, q_ref, kp_ref, k0_ref, kn_ref, vp_ref, v0_ref, vn_ref, kc_ref, vc_ref,
                 cq_ref, sq_ref, cp_ref, sp_ref, cn_ref, sn_ref, g_ref, o_ref, *, nb_lat, nb_seq):
    t = pl.program_id(0)
    is_lat = t < nb_lat
    n = t % nb_seq
    has_prev = jnp.logical_and(is_lat, n > 0)
    has_next = jnp.logical_and(is_lat, n < nb_seq - 1)
    scale = HEAD_DIM ** -0.5

    lane = lax.broadcasted_iota(jnp.int32, (BLOCK, HEAD_DIM), 1)
    quarter = lane // (HEAD_DIM // 4)
    fot = jnp.logical_or(quarter == 0, quarter == 2)

    cq, sq = cq_ref[...], sq_ref[...]
    cp, sp = cp_ref[...], sp_ref[...]
    cn, sn = cn_ref[...], sn_ref[...]

    qi = lax.broadcasted_iota(jnp.int32, (BLOCK, 3 * BLOCK), 0)
    kj = lax.broadcasted_iota(jnp.int32, (BLOCK, 3 * BLOCK), 1)
    blk = kj // BLOCK
    rel = kj - BLOCK - qi
    lo_blk = jnp.where(is_lat, jnp.where(has_prev, 0, 1), 3)
    hi_blk = jnp.where(has_next, 2, 1)
    valid = (jnp.abs(rel) <= WINDOW) & (blk >= lo_blk) & (blk <= hi_blk)
    bias = jnp.where(valid, 0.0, NEG_BIG)
    bias4 = jnp.concatenate([bias] * Q_PER_KV, axis=0)

    outs = []
    for g in range(N_KV_HEADS):
        ksl = slice(g * HEAD_DIM, (g + 1) * HEAD_DIM)
        kw = jnp.concatenate([
            _rope(kp_ref[:, ksl], cp, sp, fot),
            _rope(k0_ref[:, ksl], cq, sq, fot),
            _rope(kn_ref[:, ksl], cn, sn, fot)], axis=0).astype(jnp.bfloat16)
        vw = jnp.concatenate([vp_ref[:, ksl], v0_ref[:, ksl], vn_ref[:, ksl]], axis=0).astype(jnp.bfloat16)
        kc = kc_ref[:, ksl].astype(jnp.bfloat16)
        vc = vc_ref[:, ksl].astype(jnp.bfloat16)
        qs = []
        sinks = []
        for r in range(Q_PER_KV):
            h = g * Q_PER_KV + r
            qs.append(_rope(q_ref[:, h * HEAD_DIM:(h + 1) * HEAD_DIM], cq, sq, fot))
            sinks.append(jnp.full((BLOCK, 1), sink_ref[h], jnp.float32))
        qg = jnp.concatenate(qs, axis=0).astype(jnp.bfloat16)
        sink = jnp.concatenate(sinks, axis=0)
        s_w = lax.dot_general(qg, kw, (((1,), (1,)), ((), ())), preferred_element_type=jnp.float32) * scale
        s_c = lax.dot_general(qg, kc, (((1,), (1,)), ((), ())), preferred_element_type=jnp.float32) * scale
        s_w = jnp.where(bias4 == 0.0, s_w, NEG_BIG)
        m = jnp.maximum(jnp.maximum(jnp.max(s_w, axis=-1, keepdims=True), jnp.max(s_c, axis=-1, keepdims=True)), sink)
        p_w = jnp.exp(s_w - m)
        p_c = jnp.exp(s_c - m)
        denom = jnp.sum(p_w, axis=-1, keepdims=True) + jnp.sum(p_c, axis=-1, keepdims=True) + jnp.exp(sink - m)
        o = (jnp.dot(p_w.astype(jnp.bfloat16), vw, preferred_element_type=jnp.float32)
             + jnp.dot(p_c.astype(jnp.bfloat16), vc, preferred_element_type=jnp.float32)) / denom
        o = o * lax.rsqrt(jnp.mean(o * o, axis=-1, keepdims=True) + LN_EPS)
        for r in range(Q_PER_KV):
            h = g * Q_PER_KV + r
            outs.append(o[r * BLOCK:(r + 1) * BLOCK, :] * g_ref[:, h * HEAD_DIM:(h + 1) * HEAD_DIM])
    o_ref[...] = jnp.concatenate(outs, axis=1)


def _attn_call(p, sink_l, g_attn_l, cos_t, sin_t, *, batch, seq, ctx_len, off_q, off_k, off_v):
    r = p.shape[0]
    nb_seq = seq // BLOCK
    nb_lat = batch * nb_seq
    nb_ctx = ctx_len // BLOCK
    attn_w = N_Q_HEADS * HEAD_DIM
    kv_w = N_KV_HEADS * HEAD_DIM
    qb, kb, vb = off_q // attn_w, off_k // kv_w, off_v // kv_w
    last = r // BLOCK - 1

    def prev_i(t):
        return jnp.maximum(t - 1, 0)

    def next_i(t):
        return jnp.minimum(t + 1, last)

    def ctx_i(t):
        b = jnp.where(t < nb_lat, t // nb_seq, (t - nb_lat) // nb_ctx)
        return (batch * seq) // ctx_len + b

    kv_spec = lambda fn, col: pl.BlockSpec((BLOCK, kv_w), lambda t: (fn(t), col))
    cs_spec = lambda fn: pl.BlockSpec((BLOCK, HEAD_DIM), lambda t: (fn(t), 0))
    ident = lambda t: t
    return pl.pallas_call(
        functools.partial(_attn_kernel, nb_lat=nb_lat, nb_seq=nb_seq),
        grid=(r // BLOCK,),
        in_specs=[
            pl.BlockSpec(memory_space=pltpu.SMEM),
            pl.BlockSpec((BLOCK, attn_w), lambda t: (t, qb)),
            kv_spec(prev_i, kb), kv_spec(ident, kb), kv_spec(next_i, kb),
            kv_spec(prev_i, vb), kv_spec(ident, vb), kv_spec(next_i, vb),
            pl.BlockSpec((ctx_len, kv_w), lambda t: (ctx_i(t), kb)),
            pl.BlockSpec((ctx_len, kv_w), lambda t: (ctx_i(t), vb)),
            cs_spec(ident), cs_spec(ident), cs_spec(prev_i), cs_spec(prev_i), cs_spec(next_i), cs_spec(next_i),
            pl.BlockSpec((1, attn_w), lambda t: (0, 0)),
        ],
        out_specs=pl.BlockSpec((BLOCK, attn_w), lambda t: (t, 0)),
        out_shape=jax.ShapeDtypeStruct((r, attn_w), jnp.float32),
        compiler_params=pltpu.CompilerParams(
            dimension_semantics=("arbitrary",), vmem_limit_bytes=VMEM_LIMIT_BYTES),
        name="attn",
    )(sink_l, p, p, p, p, p, p, p, p, p, cos_t, sin_t, cos_t, sin_t, cos_t, sin_t, g_attn_l.reshape(1, attn_w))


def _out_kernel(ya_ref, gb_ref, gc_ref, xt_ref, gcp_ref, xtp_ref, gcn_ref, xtn_ref, cw_ref, gconv_ref,
                w_ref, x_ref, mod_ref, lng_ref, lnb_ref, x1_ref, h2_ref, *, alpha, first_fn, last_fn, conv_w):
    i = pl.program_id(0)
    r0 = i * TM_OUT
    is_first = first_fn(r0)
    is_last = last_fn(r0)
    u = gc_ref[...] * xt_ref[...]
    u_prev_row = jnp.where(is_first, 0.0, gcp_ref[SUBLANES - 1:SUBLANES, :] * xtp_ref[SUBLANES - 1:SUBLANES, :])
    u_next_row = jnp.where(is_last, 0.0, gcn_ref[0:1, :] * xtn_ref[0:1, :])
    row = lax.broadcasted_iota(jnp.int32, u.shape, 0)
    up = jnp.where(row == 0, u_prev_row, pltpu.roll(u, 1, axis=0))
    un = jnp.where(row == TM_OUT - 1, u_next_row, pltpu.roll(u, TM_OUT - 1, axis=0))
    yc = gb_ref[...] * (up * cw_ref[0:1, :] + u * cw_ref[1:2, :] + un * cw_ref[2:3, :])
    parts = []
    for g in range(conv_w // HEAD_DIM):
        blk = yc[:, g * HEAD_DIM:(g + 1) * HEAD_DIM]
        parts.append(blk * lax.rsqrt(jnp.mean(blk * blk, axis=-1, keepdims=True) + LN_EPS))
    ycn = jnp.concatenate(parts, axis=1) * gconv_ref[...]
    cat = jnp.concatenate([ya_ref[...], ycn], axis=1).astype(jnp.bfloat16)
    y = jnp.dot(cat, w_ref[...], preferred_element_type=jnp.float32)
    ga1 = mod_ref[0, 2:3, :]
    sh2 = mod_ref[0, 3:4, :]
    sc2 = mod_ref[0, 4:5, :]
    x1 = _layer_norm(alpha * x_ref[...] + ga1 * y) * lng_ref[...] + lnb_ref[...]
    x1_ref[...] = x1
    h2_ref[...] = _layer_norm(x1) * (1.0 + sc2) + sh2


def _out_call(ya, p, conv_w_l, g_conv_l, w_out_bf, x_all, mod_l, ln_g, ln_b, *, alpha, mod_idx_fn, first_fn, last_fn,
              conv_w):
    r, d = x_all.shape
    nblk8 = r // SUBLANES
    hb = TM_OUT // SUBLANES

    def prev8(i):
        return jnp.maximum(i * hb - 1, 0)

    def next8(i):
        return jnp.minimum((i + 1) * hb, nblk8 - 1)

    tile = lambda col: pl.BlockSpec((TM_OUT, conv_w), lambda i: (i, col))
    halo = lambda fn, col: pl.BlockSpec((SUBLANES, conv_w), lambda i: (fn(i), col))
    vec = lambda w: pl.BlockSpec((1, w), lambda i: (0, 0))
    return pl.pallas_call(
        functools.partial(_out_kernel, alpha=alpha, first_fn=first_fn, last_fn=last_fn, conv_w=conv_w),
        grid=(r // TM_OUT,),
        in_specs=[
            pl.BlockSpec((TM_OUT, ya.shape[1]), lambda i: (i, 0)),
            tile(0), tile(1), tile(2),
            halo(prev8, 1), halo(prev8, 2), halo(next8, 1), halo(next8, 2),
            pl.BlockSpec((CONV_K, conv_w), lambda i: (0, 0)),
            vec(conv_w),
            pl.BlockSpec(w_out_bf.shape, lambda i: (0, 0)),
            pl.BlockSpec((TM_OUT, d), lambda i: (i, 0)),
            pl.BlockSpec((1, 6, d), lambda i: (mod_idx_fn(i * TM_OUT), 0, 0)),
            vec(d), vec(d),
        ],
        out_specs=[pl.BlockSpec((TM_OUT, d), lambda i: (i, 0)), pl.BlockSpec((TM_OUT, d), lambda i: (i, 0))],
        out_shape=[jax.ShapeDtypeStruct((r, d), jnp.float32), jax.ShapeDtypeStruct((r, d), jnp.float32)],
        compiler_params=pltpu.CompilerParams(
            dimension_semantics=("arbitrary",), vmem_limit_bytes=VMEM_LIMIT_BYTES),
        name="conv_out_proj",
    )(ya, p, p, p, p, p, p, p, conv_w_l, g_conv_l.reshape(1, conv_w), w_out_bf, x_all, mod_l,
      ln_g.reshape(1, d), ln_b.reshape(1, d))


def _peer_score_kernel(h_ref, wq_ref, keys_ref, s_ref):
    q = jnp.dot(h_ref[...].astype(jnp.bfloat16), wq_ref[...], preferred_element_type=jnp.float32)
    qb = q.astype(jnp.bfloat16)
    parts = []
    for hp in range(2 * PEER_HEADS):
        parts.append(lax.dot_general(qb[:, hp * PEER_HALF:(hp + 1) * PEER_HALF], keys_ref[hp],
                                     (((1,), (1,)), ((), ())), preferred_element_type=jnp.float32))
    s_ref[...] = jnp.concatenate(parts, axis=1)


def _peer_score_call(h2, wq_bf, keys_bf):
    r, d = h2.shape
    nq = wq_bf.shape[1]
    return pl.pallas_call(
        _peer_score_kernel,
        grid=(r // TM_PQ,),
        in_specs=[
            pl.BlockSpec((TM_PQ, d), lambda i: (i, 0)),
            pl.BlockSpec((d, nq), lambda i: (0, 0)),
            pl.BlockSpec(keys_bf.shape, lambda i: (0, 0, 0)),
        ],
        out_specs=pl.BlockSpec((TM_PQ, nq), lambda i: (i, 0)),
        out_shape=jax.ShapeDtypeStruct((r, 2 * PEER_HEADS * N_KEYS), jnp.float32),
        compiler_params=pltpu.CompilerParams(
            dimension_semantics=("arbitrary",), vmem_limit_bytes=VMEM_LIMIT_BYTES),
        name="peer_scores",
    )(h2, wq_bf, keys_bf)


def _extract_max(x, lane):
    m = jnp.max(x, axis=-1, keepdims=True)
    pos = jnp.min(jnp.where(x == m, lane, x.shape[-1]), axis=-1, keepdims=True)
    sel = lane == pos
    return m, pos, sel, jnp.where(sel, -jnp.inf, x)


def _topk_kernel(s_ref, idx_ref, gate_ref):
    tk = s_ref.shape[0]
    k = PEER_TOPK
    lane = lax.broadcasted_iota(jnp.int32, (tk, N_KEYS), 1)
    lane2 = lax.broadcasted_iota(jnp.int32, (tk, k * k), 1)
    out_s = jnp.zeros((tk, PEER_HEADS * k), jnp.float32)
    out_i = jnp.zeros((tk, PEER_HEADS * k), jnp.int32)
    max_rep = jnp.zeros((tk, PEER_HEADS * k), jnp.float32)
    for h in range(PEER_HEADS):
        x1 = s_ref[:, (2 * h) * N_KEYS:(2 * h + 1) * N_KEYS]
        x2 = s_ref[:, (2 * h + 1) * N_KEYS:(2 * h + 2) * N_KEYS]
        cs = jnp.zeros((tk, k * k), jnp.float32)
        ci = jnp.zeros((tk, k * k), jnp.int32)
        for a in range(k):
            m1, p1, _, x1 = _extract_max(x1, lane)
            m2, p2, _, x2 = _extract_max(x2, lane)
            in_a = lane2 // k == a
            in_b = lane2 % k == a
            cs = cs + jnp.where(in_a, m1, 0.0) + jnp.where(in_b, m2, 0.0)
            ci = ci + jnp.where(in_a, p1 * N_KEYS, 0) + jnp.where(in_b, p2, 0)
        for it in range(k):
            m, _, sel, cs = _extract_max(cs, lane2)
            e = jnp.sum(jnp.where(sel, ci, 0), axis=-1, keepdims=True)
            here = lane == h * k + it
            out_s = jnp.where(here, m, out_s)
            out_i = jnp.where(here, e, out_i)
            if it == 0:
                max_rep = jnp.where(lane // k == h, m, max_rep)
    ex = jnp.exp(out_s - max_rep)
    den = jnp.zeros_like(ex)
    for h in range(PEER_HEADS):
        grp = lane // k == h
        den = jnp.where(grp, jnp.sum(jnp.where(grp, ex, 0.0), axis=-1, keepdims=True), den)
    idx_ref[...] = out_i
    gate_ref[...] = ex / den


def _topk_call(s):
    r = s.shape[0]
    w = PEER_HEADS * PEER_TOPK
    return pl.pallas_call(
        _topk_kernel,
        grid=(r // TK_TOPK,),
        in_specs=[pl.BlockSpec((TK_TOPK, s.shape[1]), lambda i: (i, 0))],
        out_specs=[pl.BlockSpec((TK_TOPK, w), lambda i: (i, 0)), pl.BlockSpec((TK_TOPK, w), lambda i: (i, 0))],
        out_shape=[jax.ShapeDtypeStruct((r, w), jnp.int32), jax.ShapeDtypeStruct((r, w), jnp.float32)],
        compiler_params=pltpu.CompilerParams(
            dimension_semantics=("arbitrary",), vmem_limit_bytes=VMEM_LIMIT_BYTES),
        name="peer_topk",
    )(s)


def _peer_kernel(idx_ref, gate_ref, h_ref, x_ref, mod_ref, lng_ref, lnb_ref, u_hbm, v_hbm, o_ref,
                 ubuf, vbuf, y_scr, sems, *, alpha, layer):
    s = pl.program_id(0)
    nblk = pl.num_programs(0) - 1
    nsel = PEER_HEADS * PEER_TOPK
    slot = s % 2

    @pl.when(s < nblk)
    def _():
        for j in range(TB_PEER):
            for k in range(nsel):
                e = idx_ref[j, k]
                dst = pl.ds(j * nsel + k, 1)
                pltpu.make_async_copy(u_hbm.at[layer, pl.ds(e, 1)], ubuf.at[slot, dst], sems.at[0, slot]).start()
                pltpu.make_async_copy(v_hbm.at[layer, pl.ds(e, 1)], vbuf.at[slot, dst], sems.at[1, slot]).start()

    @pl.when(s > 0)
    def _():
        ps = 1 - slot
        pltpu.make_async_copy(u_hbm.at[layer, pl.ds(0, TB_PEER * nsel)], ubuf.at[ps], sems.at[0, ps]).wait()
        pltpu.make_async_copy(v_hbm.at[layer, pl.ds(0, TB_PEER * nsel)], vbuf.at[ps], sems.at[1, ps]).wait()
        gate_t = gate_ref[...].T
        for j in range(TB_PEER):
            rows = pl.ds(j * nsel, nsel)
            a = jnp.sum(ubuf[ps, rows, :] * h_ref[j:j + 1, :], axis=-1, keepdims=True)
            act = 0.5 * a * (1.0 + lax.erf(a * (1.0 / math.sqrt(2.0))))
            wgt = gate_t[:, j:j + 1] * act
            y_scr[j:j + 1, :] = jnp.sum(vbuf[ps, rows, :] * wgt, axis=0, keepdims=True)
        ga2 = mod_ref[0, 5:6, :]
        o_ref[...] = _layer_norm(alpha * x_ref[...] + ga2 * y_scr[...]) * lng_ref[...] + lnb_ref[...]


def _peer_call(idx, gate, h2, x1, mod_l, ln_g, ln_b, u_tab, v_tab, *, alpha, layer, mod_idx_fn):
    r, d = h2.shape
    nsel = PEER_HEADS * PEER_TOPK
    nblk = r // TB_PEER
    cur = lambda s: jnp.minimum(s, nblk - 1)
    prv = lambda s: jnp.maximum(s - 1, 0)
    return pl.pallas_call(
        functools.partial(_peer_kernel, alpha=alpha, layer=layer),
        grid=(nblk + 1,),
        in_specs=[
            pl.BlockSpec((TB_PEER, nsel), lambda s: (cur(s), 0), memory_space=pltpu.SMEM),
            pl.BlockSpec((TB_PEER, nsel), lambda s: (prv(s), 0)),
            pl.BlockSpec((TB_PEER, d), lambda s: (prv(s), 0)),
            pl.BlockSpec((TB_PEER, d), lambda s: (prv(s), 0)),
            pl.BlockSpec((1, 6, d), lambda s: (mod_idx_fn(prv(s) * TB_PEER), 0, 0)),
            pl.BlockSpec((1, d), lambda s: (0, 0)),
            pl.BlockSpec((1, d), lambda s: (0, 0)),
            pl.BlockSpec(memory_space=pl.ANY),
            pl.BlockSpec(memory_space=pl.ANY),
        ],
        out_specs=pl.BlockSpec((TB_PEER, d), lambda s: (prv(s), 0)),
        out_shape=jax.ShapeDtypeStruct((r, d), jnp.float32),
        scratch_shapes=[
            pltpu.VMEM((2, TB_PEER * nsel, d), jnp.float32),
            pltpu.VMEM((2, TB_PEER * nsel, d), jnp.float32),
            pltpu.VMEM((TB_PEER, d), jnp.float32),
            pltpu.SemaphoreType.DMA((2, 2)),
        ],
        compiler_params=pltpu.CompilerParams(
            dimension_semantics=("arbitrary",), vmem_limit_bytes=VMEM_LIMIT_BYTES),
        name="peer_experts",
    )(idx, gate, h2, x1, mod_l, ln_g.reshape(1, d), ln_b.reshape(1, d), u_tab, v_tab)


def _rope_tables(batch, seq, n_ctx_rows):
    rows = seq // GRID_W
    row = jnp.broadcast_to(jnp.arange(rows)[:, None], (rows, GRID_W)).reshape(-1)
    col = jnp.broadcast_to(jnp.arange(GRID_W)[None, :], (rows, GRID_W)).reshape(-1)
    inv = ROPE_BASE ** (-jnp.arange(0, ROT_HALF, 2, dtype=jnp.float32) / ROT_HALF)
    ar = row.astype(jnp.float32)[:, None] * inv
    ac = col.astype(jnp.float32)[:, None] * inv
    ang = jnp.concatenate([ar, ar, ac, ac], axis=-1)
    cos, sin = jnp.cos(ang), jnp.sin(ang)
    quarter = jnp.arange(HEAD_DIM) // (HEAD_DIM // 4)
    sign = jnp.where((quarter == 0) | (quarter == 2), -1.0, 1.0).astype(jnp.float32)
    sin_signed = sin * sign
    cos_t = jnp.concatenate([jnp.tile(cos, (batch, 1)), jnp.ones((n_ctx_rows, HEAD_DIM), jnp.float32)], axis=0)
    sin_t = jnp.concatenate([jnp.tile(sin_signed, (batch, 1)), jnp.zeros((n_ctx_rows, HEAD_DIM), jnp.float32)], axis=0)
    return cos_t, sin_t


def kernel(x, c, ctx, c_ctx, w_ada, b_ada, w_in, conv_w, attn_sink, g_attn, g_conv, w_out, ln1_g, ln1_b,
           peer_wq, peer_keys, peer_u, peer_v, ln2_g, ln2_b):
    batch, seq, d = x.shape
    ctx_len = ctx.shape[1]
    depth = w_ada.shape[0]
    conv_width = conv_w.shape[2]
    attn_w = N_Q_HEADS * HEAD_DIM
    kv_w = N_KV_HEADS * HEAD_DIM
    off_q = 3 * conv_width
    off_k = off_q + attn_w
    off_v = off_k + kv_w
    n_lat = batch * seq
    r = n_lat + batch * ctx_len
    alpha = (2.0 * depth) ** 0.25
    assert batch + 1 <= SUBLANES and seq % TM_IN == 0 and (batch * ctx_len) % TM_IN == 0
    assert ctx_len == TM_OUT and seq % TM_OUT == 0 and ctx_len % BLOCK == 0 and seq % GRID_W == 0
    assert r % TB_PEER == 0 and seq % TB_PEER == 0 and r % TK_TOPK == 0
    assert w_in.shape[2] % TN_IN == 0 and w_ada.shape[2] % TN_ADA == 0 and r % TM_PQ == 0

    def mod_idx_fn(row0):
        return jnp.where(row0 < n_lat, row0 // seq, batch)

    def first_fn(row0):
        return jnp.where(row0 < n_lat, row0 % seq == 0, (row0 - n_lat) % ctx_len == 0)

    def last_fn(row0):
        end = row0 + TM_OUT
        return jnp.where(row0 < n_lat, end % seq == 0, (end - n_lat) % ctx_len == 0)

    x_all = jnp.concatenate([x.reshape(n_lat, d), ctx.reshape(batch * ctx_len, d)], axis=0)
    cond_rows = jnp.concatenate([c, c_ctx[None, :], jnp.zeros((SUBLANES - batch - 1, d), jnp.float32)], axis=0)
    mod = _ada_call(cond_rows.T, w_ada, b_ada, batch + 1).reshape(depth, SUBLANES, 6, d)
    cos_t, sin_t = _rope_tables(batch, seq, batch * ctx_len)

    for layer in range(depth):
        mod_l = mod[layer]
        p = _in_proj_call(x_all, mod_l, w_in[layer].astype(jnp.bfloat16), mod_idx_fn)
        ya = _attn_call(p, attn_sink[layer], g_attn[layer], cos_t, sin_t, batch=batch, seq=seq, ctx_len=ctx_len,
                        off_q=off_q, off_k=off_k, off_v=off_v)
        x1, h2 = _out_call(ya, p, conv_w[layer], g_conv[layer], w_out[layer].astype(jnp.bfloat16), x_all, mod_l,
                           ln1_g[layer], ln1_b[layer], alpha=alpha, mod_idx_fn=mod_idx_fn, first_fn=first_fn,
                           last_fn=last_fn, conv_w=conv_width)
        keys_bf = peer_keys[layer].reshape(2 * PEER_HEADS, N_KEYS, PEER_HALF).astype(jnp.bfloat16)
        s = _peer_score_call(h2, peer_wq[layer].astype(jnp.bfloat16), keys_bf)
        idx, gate = _topk_call(s)
        x_all = _peer_call(idx, gate, h2, x1, mod_l, ln2_g[layer], ln2_b[layer], peer_u, peer_v,
                           alpha=alpha, layer=layer, mod_idx_fn=mod_idx_fn)
    return x_all[:n_lat].reshape(batch, seq, d)
```

```python
import functools
import math

import jax
import jax.numpy as jnp
from jax import lax
from jax.experimental import pallas as pl
from jax.experimental.pallas import tpu as pltpu

HEAD_DIM = 128
N_Q_HEADS = 8
N_KV_HEADS = 2
Q_PER_KV = N_Q_HEADS // N_KV_HEADS
GRID_W = 64
WINDOW = 128
BLOCK = 128
ROPE_BASE = 10000.0
ROT_HALF = HEAD_DIM // 2
CONV_K = 3
PEER_HEADS = 8
N_KEYS = 128
PEER_TOPK = 16
PEER_HALF = 128
LN_EPS = 1e-6
NEG_BIG = -1e30

LANES = 128
SUBLANES = 8
VMEM_LIMIT_BYTES = 56 * 1024 * 1024

TM_IN = 512
TN_IN = 1536
TM_OUT = 256
TM_PQ = 256
TK_TOPK = 128
TB_PEER = 8
TN_ADA = 512


def _layer_norm(x):
    mu = jnp.mean(x, axis=-1, keepdims=True)
    xc = x - mu
    var = jnp.mean(xc * xc, axis=-1, keepdims=True)
    return xc * lax.rsqrt(var + LN_EPS)


def _ada_kernel(condt_ref, w_ref, b_ref, o_ref, *, n_rows):
    ct = condt_ref[...]
    ct = ct * (1.0 / (1.0 + jnp.exp(-ct)))
    w = w_ref[0]
    rows = []
    for r in range(SUBLANES):
        if r < n_rows:
            rows.append(jnp.sum(w * ct[:, r:r + 1], axis=0, keepdims=True) + b_ref[0])
        else:
            rows.append(jnp.zeros_like(b_ref[0]))
    o_ref[0] = jnp.concatenate(rows, axis=0)


def _ada_call(condt, w_ada, b_ada, n_rows):
    depth, d, n6 = w_ada.shape
    return pl.pallas_call(
        functools.partial(_ada_kernel, n_rows=n_rows),
        grid=(depth, n6 // TN_ADA),
        in_specs=[
            pl.BlockSpec((d, SUBLANES), lambda l, j: (0, 0)),
            pl.BlockSpec((1, d, TN_ADA), lambda l, j: (l, 0, j)),
            pl.BlockSpec((1, 1, TN_ADA), lambda l, j: (l, 0, j)),
        ],
        out_specs=pl.BlockSpec((1, SUBLANES, TN_ADA), lambda l, j: (l, 0, j)),
        out_shape=jax.ShapeDtypeStruct((depth, SUBLANES, n6), jnp.float32),
        compiler_params=pltpu.CompilerParams(
            dimension_semantics=("arbitrary", "arbitrary"), vmem_limit_bytes=VMEM_LIMIT_BYTES),
        name="ada_mod",
    )(condt, w_ada, b_ada.reshape(depth, 1, n6))


def _in_proj_kernel(x_ref, mod_ref, w_ref, o_ref, h_scr):
    @pl.when(pl.program_id(1) == 0)
    def _():
        sh = mod_ref[0, 0:1, :]
        sc = mod_ref[0, 1:2, :]
        h_scr[...] = (_layer_norm(x_ref[...]) * (1.0 + sc) + sh).astype(jnp.bfloat16)

    o_ref[...] = jnp.dot(h_scr[...], w_ref[...], preferred_element_type=jnp.float32)


def _in_proj_call(x_all, mod_l, w_in_bf, mod_idx_fn):
    r, d = x_all.shape
    n = w_in_bf.shape[1]
    return pl.pallas_call(
        _in_proj_kernel,
        grid=(r // TM_IN, n // TN_IN),
        in_specs=[
            pl.BlockSpec((TM_IN, d), lambda i, j: (i, 0)),
            pl.BlockSpec((1, 6, d), lambda i, j: (mod_idx_fn(i * TM_IN), 0, 0)),
            pl.BlockSpec((d, TN_IN), lambda i, j: (0, j)),
        ],
        out_specs=pl.BlockSpec((TM_IN, TN_IN), lambda i, j: (i, j)),
        out_shape=jax.ShapeDtypeStruct((r, n), jnp.float32),
        scratch_shapes=[pltpu.VMEM((TM_IN, d), jnp.bfloat16)],
        compiler_params=pltpu.CompilerParams(
            dimension_semantics=("arbitrary", "arbitrary"), vmem_limit_bytes=VMEM_LIMIT_BYTES),
        name="in_proj",
    )(x_all, mod_l, w_in_bf)


def _rope(x, cos, sin_signed, first_or_third):
    rot = jnp.where(first_or_third, pltpu.roll(x, 3 * HEAD_DIM // 4, axis=1), pltpu.roll(x, HEAD_DIM // 4, axis=1))
    return x * cos + rot * sin_signed


def _attn_kernel(sink_ref, q_ref, kp_ref, k0_ref, kn_ref, vp_ref, v0_ref, vn_ref, kc_ref, vc_ref,
                 cq_ref, sq_ref, cp_ref, sp_ref, cn_ref, sn_ref, g_ref, o_ref, *, nb_lat, nb_seq):
    t = pl.program_id(0)
    is_lat = t < nb_lat
    n = t % nb_seq
    has_prev = jnp.logical_and(is_lat, n > 0)
    has_next = jnp.logical_and(is_lat, n < nb_seq - 1)
    scale = HEAD_DIM ** -0.5

    lane = lax.broadcasted_iota(jnp.int32, (BLOCK, HEAD_DIM), 1)
    quarter = lane // (HEAD_DIM // 4)
    fot = jnp.logical_or(quarter == 0, quarter == 2)

    cq, sq = cq_ref[...], sq_ref[...]
    cp, sp = cp_ref[...], sp_ref[...]
    cn, sn = cn_ref[...], sn_ref[...]

    qi = lax.broadcasted_iota(jnp.int32, (BLOCK, 3 * BLOCK), 0)
    kj = lax.broadcasted_iota(jnp.int32, (BLOCK, 3 * BLOCK), 1)
    blk = kj // BLOCK
    rel = kj - BLOCK - qi
    lo_blk = jnp.where(is_lat, jnp.where(has_prev, 0, 1), 3)
    hi_blk = jnp.where(has_next, 2, 1)
    valid = (jnp.abs(rel) <= WINDOW) & (blk >= lo_blk) & (blk <= hi_blk)
    bias = jnp.where(valid, 0.0, NEG_BIG)
    bias4 = jnp.concatenate([bias] * Q_PER_KV, axis=0)

    outs = []
    for g in range(N_KV_HEADS):
        ksl = slice(g * HEAD_DIM, (g + 1) * HEAD_DIM)
        kw = jnp.concatenate([
            _rope(kp_ref[:, ksl], cp, sp, fot),
            _rope(k0_ref[:, ksl], cq, sq, fot),
            _rope(kn_ref[:, ksl], cn, sn, fot)], axis=0).astype(jnp.bfloat16)
        vw = jnp.concatenate([vp_ref[:, ksl], v0_ref[:, ksl], vn_ref[:, ksl]], axis=0).astype(jnp.bfloat16)
        kc = kc_ref[:, ksl].astype(jnp.bfloat16)
        vc = vc_ref[:, ksl].astype(jnp.bfloat16)
        qs = []
        sinks = []
        for r in range(Q_PER_KV):
            h = g * Q_PER_KV + r
            qs.append(_rope(q_ref[:, h * HEAD_DIM:(h + 1) * HEAD_DIM], cq, sq, fot))
            sinks.append(jnp.full((BLOCK, 1), sink_ref[h], jnp.float32))
        qg = jnp.concatenate(qs, axis=0).astype(jnp.bfloat16)
        sink = jnp.concatenate(sinks, axis=0)
        s_w = lax.dot_general(qg, kw, (((1,), (1,)), ((), ())), preferred_element_type=jnp.float32) * scale
        s_c = lax.dot_general(qg, kc, (((1,), (1,)), ((), ())), preferred_element_type=jnp.float32) * scale
        s_w = jnp.where(bias4 == 0.0, s_w, NEG_BIG)
        m = jnp.maximum(jnp.maximum(jnp.max(s_w, axis=-1, keepdims=True), jnp.max(s_c, axis=-1, keepdims=True)), sink)
        p_w = jnp.exp(s_w - m)
        p_c = jnp.exp(s_c - m)
        denom = jnp.sum(p_w, axis=-1, keepdims=True) + jnp.sum(p_c, axis=-1, keepdims=True) + jnp.exp(sink - m)
        o = (jnp.dot(p_w.astype(jnp.bfloat16), vw, preferred_element_type=jnp.float32)
             + jnp.dot(p_c.astype(jnp.bfloat16), vc, preferred_element_type=jnp.float32)) / denom
        o = o * lax.rsqrt(jnp.mean(o * o, axis=-1, keepdims=True) + LN_EPS)
        for r in range(Q_PER_KV):
            h = g * Q_PER_KV + r
            outs.append(o[r * BLOCK:(r + 1) * BLOCK, :] * g_ref[:, h * HEAD_DIM:(h + 1) * HEAD_DIM])
    o_ref[...] = jnp.concatenate(outs, axis=1)


def _attn_call(p, sink_l, g_attn_l, cos_t, sin_t, *, batch, seq, ctx_len, off_q, off_k, off_v):
    r = p.shape[0]
    nb_seq = seq // BLOCK
    nb_lat = batch * nb_seq
    nb_ctx = ctx_len // BLOCK
    attn_w = N_Q_HEADS * HEAD_DIM
    kv_w = N_KV_HEADS * HEAD_DIM
    qb, kb, vb = off_q // attn_w, off_k // kv_w, off_v // kv_w
    last = r // BLOCK - 1

    def prev_i(t):
        return jnp.maximum(t - 1, 0)

    def next_i(t):
        return jnp.minimum(t + 1, last)

    def ctx_i(t):
        b = jnp.where(t < nb_lat, t // nb_seq, (t - nb_lat) // nb_ctx)
        return (batch * seq) // ctx_len + b

    kv_spec = lambda fn, col: pl.BlockSpec((BLOCK, kv_w), lambda t: (fn(t), col))
    cs_spec = lambda fn: pl.BlockSpec((BLOCK, HEAD_DIM), lambda t: (fn(t), 0))
    ident = lambda t: t
    return pl.pallas_call(
        functools.partial(_attn_kernel, nb_lat=nb_lat, nb_seq=nb_seq),
        grid=(r // BLOCK,),
        in_specs=[
            pl.BlockSpec(memory_space=pltpu.SMEM),
            pl.BlockSpec((BLOCK, attn_w), lambda t: (t, qb)),
            kv_spec(prev_i, kb), kv_spec(ident, kb), kv_spec(next_i, kb),
            kv_spec(prev_i, vb), kv_spec(ident, vb), kv_spec(next_i, vb),
            pl.BlockSpec((ctx_len, kv_w), lambda t: (ctx_i(t), kb)),
            pl.BlockSpec((ctx_len, kv_w), lambda t: (ctx_i(t), vb)),
            cs_spec(ident), cs_spec(ident), cs_spec(prev_i), cs_spec(prev_i), cs_spec(next_i), cs_spec(next_i),
            pl.BlockSpec((1, attn_w), lambda t: (0, 0)),
        ],
        out_specs=pl.BlockSpec((BLOCK, attn_w), lambda t: (t, 0)),
        out_shape=jax.ShapeDtypeStruct((r, attn_w), jnp.float32),
        compiler_params=pltpu.CompilerParams(
            dimension_semantics=("arbitrary",), vmem_limit_bytes=VMEM_LIMIT_BYTES),
        name="attn",
    )(sink_l, p, p, p, p, p, p, p, p, p, cos_t, sin_t, cos_t, sin_t, cos_t, sin_t, g_attn_l.reshape(1, attn_w))


def _out_kernel(ya_ref, gb_ref, gc_ref, xt_ref, gcp_ref, xtp_ref, gcn_ref, xtn_ref, cw_ref, gconv_ref,
                w_ref, x_ref, mod_ref, lng_ref, lnb_ref, x1_ref, h2_ref, *, alpha, first_fn, last_fn, conv_w):
    i = pl.program_id(0)
    r0 = i * TM_OUT
    is_first = first_fn(r0)
    is_last = last_fn(r0)
    u = gc_ref[...] * xt_ref[...]
    u_prev_row = jnp.where(is_first, 0.0, gcp_ref[SUBLANES - 1:SUBLANES, :] * xtp_ref[SUBLANES - 1:SUBLANES, :])
    u_next_row = jnp.where(is_last, 0.0, gcn_ref[0:1, :] * xtn_ref[0:1, :])
    row = lax.broadcasted_iota(jnp.int32, u.shape, 0)
    up = jnp.where(row == 0, u_prev_row, pltpu.roll(u, 1, axis=0))
    un = jnp.where(row == TM_OUT - 1, u_next_row, pltpu.roll(u, TM_OUT - 1, axis=0))
    yc = gb_ref[...] * (up * cw_ref[0:1, :] + u * cw_ref[1:2, :] + un * cw_ref[2:3, :])
    parts = []
    for g in range(conv_w // HEAD_DIM):
        blk = yc[:, g * HEAD_DIM:(g + 1) * HEAD_DIM]
        parts.append(blk * lax.rsqrt(jnp.mean(blk * blk, axis=-1, keepdims=True) + LN_EPS))
    ycn = jnp.concatenate(parts, axis=1) * gconv_ref[...]
    cat = jnp.concatenate([ya_ref[...], ycn], axis=1).astype(jnp.bfloat16)
    y = jnp.dot(cat, w_ref[...], preferred_element_type=jnp.float32)
    ga1 = mod_ref[0, 2:3, :]
    sh2 = mod_ref[0, 3:4, :]
    sc2 = mod_ref[0, 4:5, :]
    x1 = _layer_norm(alpha * x_ref[...] + ga1 * y) * lng_ref[...] + lnb_ref[...]
    x1_ref[...] = x1
    h2_ref[...] = _layer_norm(x1) * (1.0 + sc2) + sh2


def _out_call(ya, p, conv_w_l, g_conv_l, w_out_bf, x_all, mod_l, ln_g, ln_b, *, alpha, mod_idx_fn, first_fn, last_fn,
              conv_w):
    r, d = x_all.shape
    nblk8 = r // SUBLANES
    hb = TM_OUT // SUBLANES

    def prev8(i):
        return jnp.maximum(i * hb - 1, 0)

    def next8(i):
        return jnp.minimum((i + 1) * hb, nblk8 - 1)

    tile = lambda col: pl.BlockSpec((TM_OUT, conv_w), lambda i: (i, col))
    halo = lambda fn, col: pl.BlockSpec((SUBLANES, conv_w), lambda i: (fn(i), col))
    vec = lambda w: pl.BlockSpec((1, w), lambda i: (0, 0))
    return pl.pallas_call(
        functools.partial(_out_kernel, alpha=alpha, first_fn=first_fn, last_fn=last_fn, conv_w=conv_w),
        grid=(r // TM_OUT,),
        in_specs=[
            pl.BlockSpec((TM_OUT, ya.shape[1]), lambda i: (i, 0)),
            tile(0), tile(1), tile(2),
            halo(prev8, 1), halo(prev8, 2), halo(next8, 1), halo(next8, 2),
            pl.BlockSpec((CONV_K, conv_w), lambda i: (0, 0)),
            vec(conv_w),
            pl.BlockSpec(w_out_bf.shape, lambda i: (0, 0)),
            pl.BlockSpec((TM_OUT, d), lambda i: (i, 0)),
            pl.BlockSpec((1, 6, d), lambda i: (mod_idx_fn(i * TM_OUT), 0, 0)),
            vec(d), vec(d),
        ],
        out_specs=[pl.BlockSpec((TM_OUT, d), lambda i: (i, 0)), pl.BlockSpec((TM_OUT, d), lambda i: (i, 0))],
        out_shape=[jax.ShapeDtypeStruct((r, d), jnp.float32), jax.ShapeDtypeStruct((r, d), jnp.float32)],
        compiler_params=pltpu.CompilerParams(
            dimension_semantics=("arbitrary",), vmem_limit_bytes=VMEM_LIMIT_BYTES),
        name="conv_out_proj",
    )(ya, p, p, p, p, p, p, p, conv_w_l, g_conv_l.reshape(1, conv_w), w_out_bf, x_all, mod_l,
      ln_g.reshape(1, d), ln_b.reshape(1, d))


def _peer_score_kernel(h_ref, wq_ref, keys_ref, s_ref):
    q = jnp.dot(h_ref[...].astype(jnp.bfloat16), wq_ref[...], preferred_element_type=jnp.float32)
    qb = q.astype(jnp.bfloat16)
    for hp in range(2 * PEER_HEADS):
        s_ref[hp] = lax.dot_general(keys_ref[hp], qb[:, hp * PEER_HALF:(hp + 1) * PEER_HALF],
                                    (((1,), (1,)), ((), ())), preferred_element_type=jnp.float32)


def _peer_score_call(h2, wq_bf, keys_bf):
    r, d = h2.shape
    nq = wq_bf.shape[1]
    return pl.pallas_call(
        _peer_score_kernel,
        grid=(r // TM_PQ,),
        in_specs=[
            pl.BlockSpec((TM_PQ, d), lambda i: (i, 0)),
            pl.BlockSpec((d, nq), lambda i: (0, 0)),
            pl.BlockSpec(keys_bf.shape, lambda i: (0, 0, 0)),
        ],
        out_specs=pl.BlockSpec((2 * PEER_HEADS, N_KEYS, TM_PQ), lambda i: (0, 0, i)),
        out_shape=jax.ShapeDtypeStruct((2 * PEER_HEADS, N_KEYS, r), jnp.float32),
        compiler_params=pltpu.CompilerParams(
            dimension_semantics=("arbitrary",), vmem_limit_bytes=VMEM_LIMIT_BYTES),
        name="peer_scores",
    )(h2, wq_bf, keys_bf)


def _select_max(x, row, sentinel):
    m = jnp.max(x, axis=0, keepdims=True)
    pos = jnp.min(jnp.where(x == m, row, sentinel), axis=0, keepdims=True)
    return m, pos, jnp.where(row == pos, -jnp.inf, x)


def _topk_kernel(s_ref, idx_ref, gate_ref):
    tt = s_ref.shape[2]
    k = PEER_TOPK
    key_row = lax.broadcasted_iota(jnp.int32, (N_KEYS, tt), 0)
    k_row = lax.broadcasted_iota(jnp.int32, (k, tt), 0)
    flat_row = lax.broadcasted_iota(jnp.int32, (k * k, tt), 0)
    gates, idxs = [], []
    for h in range(PEER_HEADS):
        x1 = s_ref[2 * h]
        x2 = s_ref[2 * h + 1]
        m1s, p1s = [], []
        s2 = jnp.zeros((k, tt), jnp.float32)
        i2 = jnp.zeros((k, tt), jnp.int32)
        for a in range(k):
            m1, p1, x1 = _select_max(x1, key_row, N_KEYS)
            m2, p2, x2 = _select_max(x2, key_row, N_KEYS)
            m1s.append(m1)
            p1s.append(p1)
            s2 = jnp.where(k_row == a, m2, s2)
            i2 = jnp.where(k_row == a, p2, i2)
        cand = jnp.concatenate([m1s[a] + s2 for a in range(k)], axis=0)
        cand_i = jnp.concatenate([p1s[a] * N_KEYS + i2 for a in range(k)], axis=0)
        top_s = jnp.zeros((k, tt), jnp.float32)
        top_i = jnp.zeros((k, tt), jnp.int32)
        m0 = None
        for it in range(k):
            m, pos, cand_next = _select_max(cand, flat_row, k * k)
            e = jnp.sum(jnp.where(flat_row == pos, cand_i, 0), axis=0, keepdims=True)
            cand = cand_next
            top_s = jnp.where(k_row == it, m, top_s)
            top_i = jnp.where(k_row == it, e, top_i)
            if it == 0:
                m0 = m
        ex = jnp.exp(top_s - m0)
        gates.append(ex / jnp.sum(ex, axis=0, keepdims=True))
        idxs.append(top_i)
    gate_ref[...] = jnp.concatenate(gates, axis=0).T
    idx_ref[...] = jnp.concatenate(idxs, axis=0).T


def _topk_call(s):
    r = s.shape[2]
    w = PEER_HEADS * PEER_TOPK
    return pl.pallas_call(
        _topk_kernel,
        grid=(r // TK_TOPK,),
        in_specs=[pl.BlockSpec((2 * PEER_HEADS, N_KEYS, TK_TOPK), lambda i: (0, 0, i))],
        out_specs=[pl.BlockSpec((TK_TOPK, w), lambda i: (i, 0)), pl.BlockSpec((TK_TOPK, w), lambda i: (i, 0))],
        out_shape=[jax.ShapeDtypeStruct((r, w), jnp.int32), jax.ShapeDtypeStruct((r, w), jnp.float32)],
        compiler_params=pltpu.CompilerParams(
            dimension_semantics=("arbitrary",), vmem_limit_bytes=VMEM_LIMIT_BYTES),
        name="peer_topk",
    )(s)


def _peer_kernel(idx_ref, gate_ref, h_ref, x_ref, mod_ref, lng_ref, lnb_ref, uv_hbm, o_ref,
                 uvbuf, y_scr, sems, *, alpha, layer):
    s = pl.program_id(0)
    nblk = pl.num_programs(0) - 1
    nsel = PEER_HEADS * PEER_TOPK
    d = h_ref.shape[1]
    slot = s % 2

    @pl.when(s < nblk)
    def _():
        for j in range(TB_PEER):
            for k in range(nsel):
                e = idx_ref[j, k]
                pltpu.make_async_copy(uv_hbm.at[layer, pl.ds(e, 1)], uvbuf.at[slot, pl.ds(j * nsel + k, 1)],
                                      sems.at[slot]).start(priority=k % 2)

    @pl.when(s > 0)
    def _():
        ps = 1 - slot
        pltpu.make_async_copy(uv_hbm.at[layer, pl.ds(0, TB_PEER * nsel)], uvbuf.at[ps], sems.at[ps]).wait()
        gate_t = gate_ref[...].T
        for j in range(TB_PEER):
            rows = pl.ds(j * nsel, nsel)
            a = jnp.sum(uvbuf[ps, rows, 0:d] * h_ref[j:j + 1, :], axis=-1, keepdims=True)
            act = 0.5 * a * (1.0 + lax.erf(a * (1.0 / math.sqrt(2.0))))
            wgt = gate_t[:, j:j + 1] * act
            y_scr[j:j + 1, :] = jnp.sum(uvbuf[ps, rows, d:2 * d] * wgt, axis=0, keepdims=True)
        ga2 = mod_ref[0, 5:6, :]
        o_ref[...] = _layer_norm(alpha * x_ref[...] + ga2 * y_scr[...]) * lng_ref[...] + lnb_ref[...]


def _peer_call(idx, gate, h2, x1, mod_l, ln_g, ln_b, uv_tab, *, alpha, layer, mod_idx_fn):
    r, d = h2.shape
    nsel = PEER_HEADS * PEER_TOPK
    nblk = r // TB_PEER
    cur = lambda s: jnp.minimum(s, nblk - 1)
    prv = lambda s: jnp.maximum(s - 1, 0)
    return pl.pallas_call(
        functools.partial(_peer_kernel, alpha=alpha, layer=layer),
        grid=(nblk + 1,),
        in_specs=[
            pl.BlockSpec((TB_PEER, nsel), lambda s: (cur(s), 0), memory_space=pltpu.SMEM),
            pl.BlockSpec((TB_PEER, nsel), lambda s: (prv(s), 0)),
            pl.BlockSpec((TB_PEER, d), lambda s: (prv(s), 0)),
            pl.BlockSpec((TB_PEER, d), lambda s: (prv(s), 0)),
            pl.BlockSpec((1, 6, d), lambda s: (mod_idx_fn(prv(s) * TB_PEER), 0, 0)),
            pl.BlockSpec((1, d), lambda s: (0, 0)),
            pl.BlockSpec((1, d), lambda s: (0, 0)),
            pl.BlockSpec(memory_space=pl.ANY),
        ],
        out_specs=pl.BlockSpec((TB_PEER, d), lambda s: (prv(s), 0)),
        out_shape=jax.ShapeDtypeStruct((r, d), jnp.float32),
        scratch_shapes=[
            pltpu.VMEM((2, TB_PEER * nsel, 2 * d), jnp.float32),
            pltpu.VMEM((TB_PEER, d), jnp.float32),
            pltpu.SemaphoreType.DMA((2,)),
        ],
        compiler_params=pltpu.CompilerParams(
            dimension_semantics=("arbitrary",), vmem_limit_bytes=VMEM_LIMIT_BYTES),
        name="peer_experts",
    )(idx, gate, h2, x1, mod_l, ln_g.reshape(1, d), ln_b.reshape(1, d), uv_tab)


def _rope_tables(batch, seq, n_ctx_rows):
    rows = seq // GRID_W
    row = jnp.broadcast_to(jnp.arange(rows)[:, None], (rows, GRID_W)).reshape(-1)
    col = jnp.broadcast_to(jnp.arange(GRID_W)[None, :], (rows, GRID_W)).reshape(-1)
    inv = ROPE_BASE ** (-jnp.arange(0, ROT_HALF, 2, dtype=jnp.float32) / ROT_HALF)
    ar = row.astype(jnp.float32)[:, None] * inv
    ac = col.astype(jnp.float32)[:, None] * inv
    ang = jnp.concatenate([ar, ar, ac, ac], axis=-1)
    cos, sin = jnp.cos(ang), jnp.sin(ang)
    quarter = jnp.arange(HEAD_DIM) // (HEAD_DIM // 4)
    sign = jnp.where((quarter == 0) | (quarter == 2), -1.0, 1.0).astype(jnp.float32)
    sin_signed = sin * sign
    cos_t = jnp.concatenate([jnp.tile(cos, (batch, 1)), jnp.ones((n_ctx_rows, HEAD_DIM), jnp.float32)], axis=0)
    sin_t = jnp.concatenate([jnp.tile(sin_signed, (batch, 1)), jnp.zeros((n_ctx_rows, HEAD_DIM), jnp.float32)], axis=0)
    return cos_t, sin_t


def kernel(x, c, ctx, c_ctx, w_ada, b_ada, w_in, conv_w, attn_sink, g_attn, g_conv, w_out, ln1_g, ln1_b,
           peer_wq, peer_keys, peer_u, peer_v, ln2_g, ln2_b):
    batch, seq, d = x.shape
    ctx_len = ctx.shape[1]
    depth = w_ada.shape[0]
    conv_width = conv_w.shape[2]
    attn_w = N_Q_HEADS * HEAD_DIM
    kv_w = N_KV_HEADS * HEAD_DIM
    off_q = 3 * conv_width
    off_k = off_q + attn_w
    off_v = off_k + kv_w
    n_lat = batch * seq
    r = n_lat + batch * ctx_len
    alpha = (2.0 * depth) ** 0.25
    assert batch + 1 <= SUBLANES and seq % TM_IN == 0 and (batch * ctx_len) % TM_IN == 0
    assert ctx_len == TM_OUT and seq % TM_OUT == 0 and ctx_len % BLOCK == 0 and seq % GRID_W == 0
    assert r % TB_PEER == 0 and seq % TB_PEER == 0 and r % TK_TOPK == 0
    assert w_in.shape[2] % TN_IN == 0 and w_ada.shape[2] % TN_ADA == 0 and r % TM_PQ == 0

    def mod_idx_fn(row0):
        return jnp.where(row0 < n_lat, row0 // seq, batch)

    def first_fn(row0):
        return jnp.where(row0 < n_lat, row0 % seq == 0, (row0 - n_lat) % ctx_len == 0)

    def last_fn(row0):
        end = row0 + TM_OUT
        return jnp.where(row0 < n_lat, end % seq == 0, (end - n_lat) % ctx_len == 0)

    x_all = jnp.concatenate([x.reshape(n_lat, d), ctx.reshape(batch * ctx_len, d)], axis=0)
    cond_rows = jnp.concatenate([c, c_ctx[None, :], jnp.zeros((SUBLANES - batch - 1, d), jnp.float32)], axis=0)
    mod = _ada_call(cond_rows.T, w_ada, b_ada, batch + 1).reshape(depth, SUBLANES, 6, d)
    cos_t, sin_t = _rope_tables(batch, seq, batch * ctx_len)
    uv_tab = jnp.concatenate([peer_u, peer_v], axis=-1)

    for layer in range(depth):
        mod_l = mod[layer]
        p = _in_proj_call(x_all, mod_l, w_in[layer].astype(jnp.bfloat16), mod_idx_fn)
        ya = _attn_call(p, attn_sink[layer], g_attn[layer], cos_t, sin_t, batch=batch, seq=seq, ctx_len=ctx_len,
                        off_q=off_q, off_k=off_k, off_v=off_v)
        x1, h2 = _out_call(ya, p, conv_w[layer], g_conv[layer], w_out[layer].astype(jnp.bfloat16), x_all, mod_l,
                           ln1_g[layer], ln1_b[layer], alpha=alpha, mod_idx_fn=mod_idx_fn, first_fn=first_fn,
                           last_fn=last_fn, conv_w=conv_width)
        keys_bf = peer_keys[layer].reshape(2 * PEER_HEADS, N_KEYS, PEER_HALF).astype(jnp.bfloat16)
        s = _peer_score_call(h2, peer_wq[layer].astype(jnp.bfloat16), keys_bf)
        idx, gate = _topk_call(s)
        x_all = _peer_call(idx, gate, h2, x1, mod_l, ln2_g[layer], ln2_b[layer], uv_tab,
                           alpha=alpha, layer=layer, mod_idx_fn=mod_idx_fn)
    return x_all[:n_lat].reshape(batch, seq, d)
```

```python
import functools
import math

import jax
import jax.numpy as jnp
from jax import lax
from jax.experimental import pallas as pl
from jax.experimental.pallas import tpu as pltpu

HEAD_DIM = 128
N_Q_HEADS = 8
N_KV_HEADS = 2
Q_PER_KV = N_Q_HEADS // N_KV_HEADS
GRID_W = 64
WINDOW = 128
BLOCK = 128
ROPE_BASE = 10000.0
ROT_HALF = HEAD_DIM // 2
CONV_K = 3
PEER_HEADS = 8
N_KEYS = 128
PEER_TOPK = 16
PEER_HALF = 128
LN_EPS = 1e-6
NEG_BIG = -1e30

LANES = 128
SUBLANES = 8
VMEM_LIMIT_BYTES = 56 * 1024 * 1024

TM_IN = 512
TN_IN = 1536
TM_OUT = 256
TM_PQ = 256
TK_TOPK = 128
TB_PEER = 8
TN_ADA = 512


def _layer_norm(x):
    mu = jnp.mean(x, axis=-1, keepdims=True)
    xc = x - mu
    var = jnp.mean(xc * xc, axis=-1, keepdims=True)
    return xc * lax.rsqrt(var + LN_EPS)


def _ada_kernel(condt_ref, w_ref, b_ref, o_ref, *, n_rows):
    ct = condt_ref[...]
    ct = ct * (1.0 / (1.0 + jnp.exp(-ct)))
    w = w_ref[0]
    rows = []
    for r in range(SUBLANES):
        if r < n_rows:
            rows.append(jnp.sum(w * ct[:, r:r + 1], axis=0, keepdims=True) + b_ref[0])
        else:
            rows.append(jnp.zeros_like(b_ref[0]))
    o_ref[0] = jnp.concatenate(rows, axis=0)


def _ada_call(condt, w_ada, b_ada, n_rows):
    depth, d, n6 = w_ada.shape
    return pl.pallas_call(
        functools.partial(_ada_kernel, n_rows=n_rows),
        grid=(depth, n6 // TN_ADA),
        in_specs=[
            pl.BlockSpec((d, SUBLANES), lambda l, j: (0, 0)),
            pl.BlockSpec((1, d, TN_ADA), lambda l, j: (l, 0, j)),
            pl.BlockSpec((1, 1, TN_ADA), lambda l, j: (l, 0, j)),
        ],
        out_specs=pl.BlockSpec((1, SUBLANES, TN_ADA), lambda l, j: (l, 0, j)),
        out_shape=jax.ShapeDtypeStruct((depth, SUBLANES, n6), jnp.float32),
        compiler_params=pltpu.CompilerParams(
            dimension_semantics=("arbitrary", "arbitrary"), vmem_limit_bytes=VMEM_LIMIT_BYTES),
        name="ada_mod",
    )(condt, w_ada, b_ada.reshape(depth, 1, n6))


def _in_proj_kernel(x_ref, mod_ref, w_ref, o_ref, h_scr):
    @pl.when(pl.program_id(1) == 0)
    def _():
        sh = mod_ref[0, 0:1, :]
        sc = mod_ref[0, 1:2, :]
        h_scr[...] = (_layer_norm(x_ref[...]) * (1.0 + sc) + sh).astype(jnp.bfloat16)

    o_ref[...] = jnp.dot(h_scr[...], w_ref[...], preferred_element_type=jnp.float32)


def _in_proj_call(x_all, mod_l, w_in_bf, mod_idx_fn):
    r, d = x_all.shape
    n = w_in_bf.shape[1]
    return pl.pallas_call(
        _in_proj_kernel,
        grid=(r // TM_IN, n // TN_IN),
        in_specs=[
            pl.BlockSpec((TM_IN, d), lambda i, j: (i, 0)),
            pl.BlockSpec((1, 6, d), lambda i, j: (mod_idx_fn(i * TM_IN), 0, 0)),
            pl.BlockSpec((d, TN_IN), lambda i, j: (0, j)),
        ],
        out_specs=pl.BlockSpec((TM_IN, TN_IN), lambda i, j: (i, j)),
        out_shape=jax.ShapeDtypeStruct((r, n), jnp.float32),
        scratch_shapes=[pltpu.VMEM((TM_IN, d), jnp.bfloat16)],
        compiler_params=pltpu.CompilerParams(
            dimension_semantics=("arbitrary", "arbitrary"), vmem_limit_bytes=VMEM_LIMIT_BYTES),
        name="in_proj",
    )(x_all, mod_l, w_in_bf)


def _rope(x, cos, sin_signed, first_or_third):
    rot = jnp.where(first_or_third, pltpu.roll(x, 3 * HEAD_DIM // 4, axis=1), pltpu.roll(x, HEAD_DIM // 4, axis=1))
    return x * cos + rot * sin_signed


def _attn_kernel(sink_ref, q_ref, kp_ref, k0_ref, kn_ref, vp_ref, v0_ref, vn_ref, kc_ref, vc_ref,
                 cq_ref, sq_ref, cp_ref, sp_ref, cn_ref, sn_ref, g_ref, o_ref, *, nb_lat, nb_seq):
    t = pl.program_id(0)
    is_lat = t < nb_lat
    n = t % nb_seq
    has_prev = jnp.logical_and(is_lat, n > 0)
    has_next = jnp.logical_and(is_lat, n < nb_seq - 1)
    scale = HEAD_DIM ** -0.5

    lane = lax.broadcasted_iota(jnp.int32, (BLOCK, HEAD_DIM), 1)
    quarter = lane // (HEAD_DIM // 4)
    fot = jnp.logical_or(quarter == 0, quarter == 2)

    cq, sq = cq_ref[...], sq_ref[...]
    cp, sp = cp_ref[...], sp_ref[...]
    cn, sn = cn_ref[...], sn_ref[...]

    qi = lax.broadcasted_iota(jnp.int32, (BLOCK, 3 * BLOCK), 0)
    kj = lax.broadcasted_iota(jnp.int32, (BLOCK, 3 * BLOCK), 1)
    blk = kj // BLOCK
    rel = kj - BLOCK - qi
    lo_blk = jnp.where(is_lat, jnp.where(has_prev, 0, 1), 3)
    hi_blk = jnp.where(has_next, 2, 1)
    valid = (jnp.abs(rel) <= WINDOW) & (blk >= lo_blk) & (blk <= hi_blk)
    bias = jnp.where(valid, 0.0, NEG_BIG)
    bias4 = jnp.concatenate([bias] * Q_PER_KV, axis=0)

    outs = []
    for g in range(N_KV_HEADS):
        ksl = slice(g * HEAD_DIM, (g + 1) * HEAD_DIM)
        kw = jnp.concatenate([
            _rope(kp_ref[:, ksl], cp, sp, fot),
            _rope(k0_ref[:, ksl], cq, sq, fot),
            _rope(kn_ref[:, ksl], cn, sn, fot)], axis=0).astype(jnp.bfloat16)
        vw = jnp.concatenate([vp_ref[:, ksl], v0_ref[:, ksl], vn_ref[:, ksl]], axis=0).astype(jnp.bfloat16)
        kc = kc_ref[:, ksl].astype(jnp.bfloat16)
        vc = vc_ref[:, ksl].astype(jnp.bfloat16)
        qs = []
        sinks = []
        for r in range(Q_PER_KV):
            h = g * Q_PER_KV + r
            qs.append(_rope(q_ref[:, h * HEAD_DIM:(h + 1) * HEAD_DIM], cq, sq, fot))
            sinks.append(jnp.full((BLOCK, 1), sink_ref[h], jnp.float32))
        qg = jnp.concatenate(qs, axis=0).astype(jnp.bfloat16)
        sink = jnp.concatenate(sinks, axis=0)
        s_w = lax.dot_general(qg, kw, (((1,), (1,)), ((), ())), preferred_element_type=jnp.float32) * scale
        s_c = lax.dot_general(qg, kc, (((1,), (1,)), ((), ())), preferred_element_type=jnp.float32) * scale
        s_w = jnp.where(bias4 == 0.0, s_w, NEG_BIG)
        m = jnp.maximum(jnp.maximum(jnp.max(s_w, axis=-1, keepdims=True), jnp.max(s_c, axis=-1, keepdims=True)), sink)
        p_w = jnp.exp(s_w - m)
        p_c = jnp.exp(s_c - m)
        denom = jnp.sum(p_w, axis=-1, keepdims=True) + jnp.sum(p_c, axis=-1, keepdims=True) + jnp.exp(sink - m)
        o = (jnp.dot(p_w.astype(jnp.bfloat16), vw, preferred_element_type=jnp.float32)
             + jnp.dot(p_c.astype(jnp.bfloat16), vc, preferred_element_type=jnp.float32)) / denom
        o = o * lax.rsqrt(jnp.mean(o * o, axis=-1, keepdims=True) + LN_EPS)
        for r in range(Q_PER_KV):
            h = g * Q_PER_KV + r
            outs.append(o[r * BLOCK:(r + 1) * BLOCK, :] * g_ref[:, h * HEAD_DIM:(h + 1) * HEAD_DIM])
    o_ref[...] = jnp.concatenate(outs, axis=1)


def _attn_call(p, sink_l, g_attn_l, cos_t, sin_t, *, batch, seq, ctx_len, off_q, off_k, off_v):
    r = p.shape[0]
    nb_seq = seq // BLOCK
    nb_lat = batch * nb_seq
    nb_ctx = ctx_len // BLOCK
    attn_w = N_Q_HEADS * HEAD_DIM
    kv_w = N_KV_HEADS * HEAD_DIM
    qb, kb, vb = off_q // attn_w, off_k // kv_w, off_v // kv_w
    last = r // BLOCK - 1

    def prev_i(t):
        return jnp.maximum(t - 1, 0)

    def next_i(t):
        return jnp.minimum(t + 1, last)

    def ctx_i(t):
        b = jnp.where(t < nb_lat, t // nb_seq, (t - nb_lat) // nb_ctx)
        return (batch * seq) // ctx_len + b

    kv_spec = lambda fn, col: pl.BlockSpec((BLOCK, kv_w), lambda t: (fn(t), col))
    cs_spec = lambda fn: pl.BlockSpec((BLOCK, HEAD_DIM), lambda t: (fn(t), 0))
    ident = lambda t: t
    return pl.pallas_call(
        functools.partial(_attn_kernel, nb_lat=nb_lat, nb_seq=nb_seq),
        grid=(r // BLOCK,),
        in_specs=[
            pl.BlockSpec(memory_space=pltpu.SMEM),
            pl.BlockSpec((BLOCK, attn_w), lambda t: (t, qb)),
            kv_spec(prev_i, kb), kv_spec(ident, kb), kv_spec(next_i, kb),
            kv_spec(prev_i, vb), kv_spec(ident, vb), kv_spec(next_i, vb),
            pl.BlockSpec((ctx_len, kv_w), lambda t: (ctx_i(t), kb)),
            pl.BlockSpec((ctx_len, kv_w), lambda t: (ctx_i(t), vb)),
            cs_spec(ident), cs_spec(ident), cs_spec(prev_i), cs_spec(prev_i), cs_spec(next_i), cs_spec(next_i),
            pl.BlockSpec((1, attn_w), lambda t: (0, 0)),
        ],
        out_specs=pl.BlockSpec((BLOCK, attn_w), lambda t: (t, 0)),
        out_shape=jax.ShapeDtypeStruct((r, attn_w), jnp.float32),
        compiler_params=pltpu.CompilerParams(
            dimension_semantics=("arbitrary",), vmem_limit_bytes=VMEM_LIMIT_BYTES),
        name="attn",
    )(sink_l, p, p, p, p, p, p, p, p, p, cos_t, sin_t, cos_t, sin_t, cos_t, sin_t, g_attn_l.reshape(1, attn_w))


def _out_kernel(ya_ref, gb_ref, gc_ref, xt_ref, gcp_ref, xtp_ref, gcn_ref, xtn_ref, cw_ref, gconv_ref,
                w_ref, x_ref, mod_ref, lng_ref, lnb_ref, x1_ref, h2_ref, *, alpha, first_fn, last_fn, conv_w):
    i = pl.program_id(0)
    r0 = i * TM_OUT
    is_first = first_fn(r0)
    is_last = last_fn(r0)
    u = gc_ref[...] * xt_ref[...]
    u_prev_row = jnp.where(is_first, 0.0, gcp_ref[SUBLANES - 1:SUBLANES, :] * xtp_ref[SUBLANES - 1:SUBLANES, :])
    u_next_row = jnp.where(is_last, 0.0, gcn_ref[0:1, :] * xtn_ref[0:1, :])
    row = lax.broadcasted_iota(jnp.int32, u.shape, 0)
    up = jnp.where(row == 0, u_prev_row, pltpu.roll(u, 1, axis=0))
    un = jnp.where(row == TM_OUT - 1, u_next_row, pltpu.roll(u, TM_OUT - 1, axis=0))
    yc = gb_ref[...] * (up * cw_ref[0:1, :] + u * cw_ref[1:2, :] + un * cw_ref[2:3, :])
    parts = []
    for g in range(conv_w // HEAD_DIM):
        blk = yc[:, g * HEAD_DIM:(g + 1) * HEAD_DIM]
        parts.append(blk * lax.rsqrt(jnp.mean(blk * blk, axis=-1, keepdims=True) + LN_EPS))
    ycn = jnp.concatenate(parts, axis=1) * gconv_ref[...]
    cat = jnp.concatenate([ya_ref[...], ycn], axis=1).astype(jnp.bfloat16)
    y = jnp.dot(cat, w_ref[...], preferred_element_type=jnp.float32)
    ga1 = mod_ref[0, 2:3, :]
    sh2 = mod_ref[0, 3:4, :]
    sc2 = mod_ref[0, 4:5, :]
    x1 = _layer_norm(alpha * x_ref[...] + ga1 * y) * lng_ref[...] + lnb_ref[...]
    x1_ref[...] = x1
    h2_ref[...] = _layer_norm(x1) * (1.0 + sc2) + sh2


def _out_call(ya, p, conv_w_l, g_conv_l, w_out_bf, x_all, mod_l, ln_g, ln_b, *, alpha, mod_idx_fn, first_fn, last_fn,
              conv_w):
    r, d = x_all.shape
    nblk8 = r // SUBLANES
    hb = TM_OUT // SUBLANES

    def prev8(i):
        return jnp.maximum(i * hb - 1, 0)

    def next8(i):
        return jnp.minimum((i + 1) * hb, nblk8 - 1)

    tile = lambda col: pl.BlockSpec((TM_OUT, conv_w), lambda i: (i, col))
    halo = lambda fn, col: pl.BlockSpec((SUBLANES, conv_w), lambda i: (fn(i), col))
    vec = lambda w: pl.BlockSpec((1, w), lambda i: (0, 0))
    return pl.pallas_call(
        functools.partial(_out_kernel, alpha=alpha, first_fn=first_fn, last_fn=last_fn, conv_w=conv_w),
        grid=(r // TM_OUT,),
        in_specs=[
            pl.BlockSpec((TM_OUT, ya.shape[1]), lambda i: (i, 0)),
            tile(0), tile(1), tile(2),
            halo(prev8, 1), halo(prev8, 2), halo(next8, 1), halo(next8, 2),
            pl.BlockSpec((CONV_K, conv_w), lambda i: (0, 0)),
            vec(conv_w),
            pl.BlockSpec(w_out_bf.shape, lambda i: (0, 0)),
            pl.BlockSpec((TM_OUT, d), lambda i: (i, 0)),
            pl.BlockSpec((1, 6, d), lambda i: (mod_idx_fn(i * TM_OUT), 0, 0)),
            vec(d), vec(d),
        ],
        out_specs=[pl.BlockSpec((TM_OUT, d), lambda i: (i, 0)), pl.BlockSpec((TM_OUT, d), lambda i: (i, 0))],
        out_shape=[jax.ShapeDtypeStruct((r, d), jnp.float32), jax.ShapeDtypeStruct((r, d), jnp.float32)],
        compiler_params=pltpu.CompilerParams(
            dimension_semantics=("arbitrary",), vmem_limit_bytes=VMEM_LIMIT_BYTES),
        name="conv_out_proj",
    )(ya, p, p, p, p, p, p, p, conv_w_l, g_conv_l.reshape(1, conv_w), w_out_bf, x_all, mod_l,
      ln_g.reshape(1, d), ln_b.reshape(1, d))


def _peer_score_kernel(h_ref, wq_ref, keys_ref, s_ref):
    q = jnp.dot(h_ref[...].astype(jnp.bfloat16), wq_ref[...], preferred_element_type=jnp.float32)
    qb = q.astype(jnp.bfloat16)
    for hp in range(2 * PEER_HEADS):
        s_ref[hp] = lax.dot_general(keys_ref[hp], qb[:, hp * PEER_HALF:(hp + 1) * PEER_HALF],
                                    (((1,), (1,)), ((), ())), preferred_element_type=jnp.float32)


def _peer_score_call(h2, wq_bf, keys_bf):
    r, d = h2.shape
    nq = wq_bf.shape[1]
    return pl.pallas_call(
        _peer_score_kernel,
        grid=(r // TM_PQ,),
        in_specs=[
            pl.BlockSpec((TM_PQ, d), lambda i: (i, 0)),
            pl.BlockSpec((d, nq), lambda i: (0, 0)),
            pl.BlockSpec(keys_bf.shape, lambda i: (0, 0, 0)),
        ],
        out_specs=pl.BlockSpec((2 * PEER_HEADS, N_KEYS, TM_PQ), lambda i: (0, 0, i)),
        out_shape=jax.ShapeDtypeStruct((2 * PEER_HEADS, N_KEYS, r), jnp.float32),
        compiler_params=pltpu.CompilerParams(
            dimension_semantics=("arbitrary",), vmem_limit_bytes=VMEM_LIMIT_BYTES),
        name="peer_scores",
    )(h2, wq_bf, keys_bf)


def _select_max(x, row, sentinel):
    m = jnp.max(x, axis=0, keepdims=True)
    pos = jnp.min(jnp.where(x == m, row, sentinel), axis=0, keepdims=True)
    return m, pos, jnp.where(row == pos, -jnp.inf, x)


def _topk_kernel(s_ref, idx_ref, gate_ref):
    tt = s_ref.shape[2]
    k = PEER_TOPK
    key_row = lax.broadcasted_iota(jnp.int32, (N_KEYS, tt), 0)
    k_row = lax.broadcasted_iota(jnp.int32, (k, tt), 0)
    flat_row = lax.broadcasted_iota(jnp.int32, (k * k, tt), 0)
    gates, idxs = [], []
    for h in range(PEER_HEADS):
        x1 = s_ref[2 * h]
        x2 = s_ref[2 * h + 1]
        m1s, p1s = [], []
        s2 = jnp.zeros((k, tt), jnp.float32)
        i2 = jnp.zeros((k, tt), jnp.int32)
        for a in range(k):
            m1, p1, x1 = _select_max(x1, key_row, N_KEYS)
            m2, p2, x2 = _select_max(x2, key_row, N_KEYS)
            m1s.append(m1)
            p1s.append(p1)
            s2 = jnp.where(k_row == a, m2, s2)
            i2 = jnp.where(k_row == a, p2, i2)
        cand = jnp.concatenate([m1s[a] + s2 for a in range(k)], axis=0)
        cand_i = jnp.concatenate([p1s[a] * N_KEYS + i2 for a in range(k)], axis=0)
        top_s = jnp.zeros((k, tt), jnp.float32)
        top_i = jnp.zeros((k, tt), jnp.int32)
        m0 = None
        for it in range(k):
            m, pos, cand_next = _select_max(cand, flat_row, k * k)
            e = jnp.sum(jnp.where(flat_row == pos, cand_i, 0), axis=0, keepdims=True)
            cand = cand_next
            top_s = jnp.where(k_row == it, m, top_s)
            top_i = jnp.where(k_row == it, e, top_i)
            if it == 0:
                m0 = m
        ex = jnp.exp(top_s - m0)
        gates.append(ex / jnp.sum(ex, axis=0, keepdims=True))
        idxs.append(top_i)
    gate_ref[...] = jnp.concatenate(gates, axis=0).T
    idx_ref[...] = jnp.concatenate(idxs, axis=0).T


def _topk_call(s):
    r = s.shape[2]
    w = PEER_HEADS * PEER_TOPK
    return pl.pallas_call(
        _topk_kernel,
        grid=(r // TK_TOPK,),
        in_specs=[pl.BlockSpec((2 * PEER_HEADS, N_KEYS, TK_TOPK), lambda i: (0, 0, i))],
        out_specs=[pl.BlockSpec((TK_TOPK, w), lambda i: (i, 0)), pl.BlockSpec((TK_TOPK, w), lambda i: (i, 0))],
        out_shape=[jax.ShapeDtypeStruct((r, w), jnp.int32), jax.ShapeDtypeStruct((r, w), jnp.float32)],
        compiler_params=pltpu.CompilerParams(
            dimension_semantics=("arbitrary",), vmem_limit_bytes=VMEM_LIMIT_BYTES),
        name="peer_topk",
    )(s)


def _peer_start_token(idx_ref, row, uv_hbm, buf, sem, j, layer):
    nsel = PEER_HEADS * PEER_TOPK
    for k in range(nsel):
        e = idx_ref[row, k]
        pltpu.make_async_copy(uv_hbm.at[layer, pl.ds(e, 1)], buf.at[pl.ds(j * nsel + k, 1)], sem).start(priority=k % 2)


def _peer_wait_block(uv_hbm, buf, sem, layer):
    pltpu.make_async_copy(uv_hbm.at[layer, pl.ds(0, buf.shape[0])], buf, sem).wait()


def _peer_token(buf, j, h_row, gate_col):
    nsel = PEER_HEADS * PEER_TOPK
    w = buf[pl.ds(j * nsel, nsel), :]
    u = lax.bitcast_convert_type(w & jnp.uint32(0xFFFF0000), jnp.float32)
    v = lax.bitcast_convert_type(w << 16, jnp.float32)
    a = jnp.sum(u * h_row, axis=-1, keepdims=True)
    act = 0.5 * a * (1.0 + lax.erf(a * (1.0 / math.sqrt(2.0))))
    return jnp.sum(v * (gate_col * act), axis=0, keepdims=True)


def _peer_kernel(idx_ref, idx_next_ref, gate_ref, h_ref, x_ref, mod_ref, lng_ref, lnb_ref, uv_hbm, o_ref,
                 buf0, buf1, y_scr, sems, *, alpha, layer):
    s = pl.program_id(0)
    tb = TB_PEER

    @pl.when(s == 0)
    def _():
        for j in range(tb):
            _peer_start_token(idx_ref, j, uv_hbm, buf0, sems.at[0], j, layer)

    gate_t = gate_ref[...].T
    _peer_wait_block(uv_hbm, buf0, sems.at[0], layer)
    for j in range(tb):
        _peer_start_token(idx_ref, tb + j, uv_hbm, buf1, sems.at[1], j, layer)
        y_scr[j:j + 1, :] = _peer_token(buf0, j, h_ref[j:j + 1, :], gate_t[:, j:j + 1])
    _peer_wait_block(uv_hbm, buf1, sems.at[1], layer)
    for j in range(tb):
        _peer_start_token(idx_next_ref, j, uv_hbm, buf0, sems.at[0], j, layer)
        y_scr[tb + j:tb + j + 1, :] = _peer_token(buf1, j, h_ref[tb + j:tb + j + 1, :], gate_t[:, tb + j:tb + j + 1])

    @pl.when(s == pl.num_programs(0) - 1)
    def _():
        _peer_wait_block(uv_hbm, buf0, sems.at[0], layer)

    ga2 = mod_ref[0, 5:6, :]
    o_ref[...] = _layer_norm(alpha * x_ref[...] + ga2 * y_scr[...]) * lng_ref[...] + lnb_ref[...]


def _peer_call(idx, gate, h2, x1, mod_l, ln_g, ln_b, uv_tab, *, alpha, layer, mod_idx_fn):
    r, d = h2.shape
    nsel = PEER_HEADS * PEER_TOPK
    rows = 2 * TB_PEER
    nstep = r // rows
    nxt = lambda s: jnp.minimum(s + 1, nstep - 1)
    return pl.pallas_call(
        functools.partial(_peer_kernel, alpha=alpha, layer=layer),
        grid=(nstep,),
        in_specs=[
            pl.BlockSpec((rows, nsel), lambda s: (s, 0), memory_space=pltpu.SMEM),
            pl.BlockSpec((rows, nsel), lambda s: (nxt(s), 0), memory_space=pltpu.SMEM),
            pl.BlockSpec((rows, nsel), lambda s: (s, 0)),
            pl.BlockSpec((rows, d), lambda s: (s, 0)),
            pl.BlockSpec((rows, d), lambda s: (s, 0)),
            pl.BlockSpec((1, 6, d), lambda s: (mod_idx_fn(s * rows), 0, 0)),
            pl.BlockSpec((1, d), lambda s: (0, 0)),
            pl.BlockSpec((1, d), lambda s: (0, 0)),
            pl.BlockSpec(memory_space=pl.ANY),
        ],
        out_specs=pl.BlockSpec((rows, d), lambda s: (s, 0)),
        out_shape=jax.ShapeDtypeStruct((r, d), jnp.float32),
        scratch_shapes=[
            pltpu.VMEM((TB_PEER * nsel, d), jnp.uint32),
            pltpu.VMEM((TB_PEER * nsel, d), jnp.uint32),
            pltpu.VMEM((rows, d), jnp.float32),
            pltpu.SemaphoreType.DMA((2,)),
        ],
        compiler_params=pltpu.CompilerParams(
            dimension_semantics=("arbitrary",), vmem_limit_bytes=VMEM_LIMIT_BYTES),
        name="peer_experts",
    )(idx, idx, gate, h2, x1, mod_l, ln_g.reshape(1, d), ln_b.reshape(1, d), uv_tab)


def _rope_tables(batch, seq, n_ctx_rows):
    rows = seq // GRID_W
    row = jnp.broadcast_to(jnp.arange(rows)[:, None], (rows, GRID_W)).reshape(-1)
    col = jnp.broadcast_to(jnp.arange(GRID_W)[None, :], (rows, GRID_W)).reshape(-1)
    inv = ROPE_BASE ** (-jnp.arange(0, ROT_HALF, 2, dtype=jnp.float32) / ROT_HALF)
    ar = row.astype(jnp.float32)[:, None] * inv
    ac = col.astype(jnp.float32)[:, None] * inv
    ang = jnp.concatenate([ar, ar, ac, ac], axis=-1)
    cos, sin = jnp.cos(ang), jnp.sin(ang)
    quarter = jnp.arange(HEAD_DIM) // (HEAD_DIM // 4)
    sign = jnp.where((quarter == 0) | (quarter == 2), -1.0, 1.0).astype(jnp.float32)
    sin_signed = sin * sign
    cos_t = jnp.concatenate([jnp.tile(cos, (batch, 1)), jnp.ones((n_ctx_rows, HEAD_DIM), jnp.float32)], axis=0)
    sin_t = jnp.concatenate([jnp.tile(sin_signed, (batch, 1)), jnp.zeros((n_ctx_rows, HEAD_DIM), jnp.float32)], axis=0)
    return cos_t, sin_t


def kernel(x, c, ctx, c_ctx, w_ada, b_ada, w_in, conv_w, attn_sink, g_attn, g_conv, w_out, ln1_g, ln1_b,
           peer_wq, peer_keys, peer_u, peer_v, ln2_g, ln2_b):
    batch, seq, d = x.shape
    ctx_len = ctx.shape[1]
    depth = w_ada.shape[0]
    conv_width = conv_w.shape[2]
    attn_w = N_Q_HEADS * HEAD_DIM
    kv_w = N_KV_HEADS * HEAD_DIM
    off_q = 3 * conv_width
    off_k = off_q + attn_w
    off_v = off_k + kv_w
    n_lat = batch * seq
    r = n_lat + batch * ctx_len
    alpha = (2.0 * depth) ** 0.25
    assert batch + 1 <= SUBLANES and seq % TM_IN == 0 and (batch * ctx_len) % TM_IN == 0
    assert ctx_len == TM_OUT and seq % TM_OUT == 0 and ctx_len % BLOCK == 0 and seq % GRID_W == 0
    assert r % (2 * TB_PEER) == 0 and seq % (2 * TB_PEER) == 0 and r % TK_TOPK == 0
    assert w_in.shape[2] % TN_IN == 0 and w_ada.shape[2] % TN_ADA == 0 and r % TM_PQ == 0

    def mod_idx_fn(row0):
        return jnp.where(row0 < n_lat, row0 // seq, batch)

    def first_fn(row0):
        return jnp.where(row0 < n_lat, row0 % seq == 0, (row0 - n_lat) % ctx_len == 0)

    def last_fn(row0):
        end = row0 + TM_OUT
        return jnp.where(row0 < n_lat, end % seq == 0, (end - n_lat) % ctx_len == 0)

    x_all = jnp.concatenate([x.reshape(n_lat, d), ctx.reshape(batch * ctx_len, d)], axis=0)
    cond_rows = jnp.concatenate([c, c_ctx[None, :], jnp.zeros((SUBLANES - batch - 1, d), jnp.float32)], axis=0)
    mod = _ada_call(cond_rows.T, w_ada, b_ada, batch + 1).reshape(depth, SUBLANES, 6, d)
    cos_t, sin_t = _rope_tables(batch, seq, batch * ctx_len)
    u_bits = lax.bitcast_convert_type(peer_u.astype(jnp.bfloat16), jnp.uint16).astype(jnp.uint32)
    v_bits = lax.bitcast_convert_type(peer_v.astype(jnp.bfloat16), jnp.uint16).astype(jnp.uint32)
    uv_tab = (u_bits << 16) | v_bits

    for layer in range(depth):
        mod_l = mod[layer]
        p = _in_proj_call(x_all, mod_l, w_in[layer].astype(jnp.bfloat16), mod_idx_fn)
        ya = _attn_call(p, attn_sink[layer], g_attn[layer], cos_t, sin_t, batch=batch, seq=seq, ctx_len=ctx_len,
                        off_q=off_q, off_k=off_k, off_v=off_v)
        x1, h2 = _out_call(ya, p, conv_w[layer], g_conv[layer], w_out[layer].astype(jnp.bfloat16), x_all, mod_l,
                           ln1_g[layer], ln1_b[layer], alpha=alpha, mod_idx_fn=mod_idx_fn, first_fn=first_fn,
                           last_fn=last_fn, conv_w=conv_width)
        keys_bf = peer_keys[layer].reshape(2 * PEER_HEADS, N_KEYS, PEER_HALF).astype(jnp.bfloat16)
        s = _peer_score_call(h2, peer_wq[layer].astype(jnp.bfloat16), keys_bf)
        idx, gate = _topk_call(s)
        x_all = _peer_call(idx, gate, h2, x1, mod_l, ln2_g[layer], ln2_b[layer], uv_tab,
                           alpha=alpha, layer=layer, mod_idx_fn=mod_idx_fn)
    return x_all[:n_lat].reshape(batch, seq, d)
```

```python
import functools
import math

import jax
import jax.numpy as jnp
from jax import lax
from jax.experimental import pallas as pl
from jax.experimental.pallas import tpu as pltpu

HEAD_DIM = 128
N_Q_HEADS = 8
N_KV_HEADS = 2
Q_PER_KV = N_Q_HEADS // N_KV_HEADS
GRID_W = 64
WINDOW = 128
BLOCK = 128
ROPE_BASE = 10000.0
ROT_HALF = HEAD_DIM // 2
CONV_K = 3
PEER_HEADS = 8
N_KEYS = 128
PEER_TOPK = 16
PEER_HALF = 128
LN_EPS = 1e-6
NEG_BIG = -1e30

LANES = 128
SUBLANES = 8
VMEM_LIMIT_BYTES = 56 * 1024 * 1024

TM_IN = 512
TN_IN = 1536
TM_OUT = 256
TM_PQ = 256
TK_TOPK = 128
TB_PEER = 8
NBUF_PEER = 4
TN_ADA = 512


def _layer_norm(x):
    mu = jnp.mean(x, axis=-1, keepdims=True)
    xc = x - mu
    var = jnp.mean(xc * xc, axis=-1, keepdims=True)
    return xc * lax.rsqrt(var + LN_EPS)


def _ada_kernel(condt_ref, w_ref, b_ref, o_ref, *, n_rows):
    ct = condt_ref[...]
    ct = ct * (1.0 / (1.0 + jnp.exp(-ct)))
    w = w_ref[0]
    rows = []
    for r in range(SUBLANES):
        if r < n_rows:
            rows.append(jnp.sum(w * ct[:, r:r + 1], axis=0, keepdims=True) + b_ref[0])
        else:
            rows.append(jnp.zeros_like(b_ref[0]))
    o_ref[0] = jnp.concatenate(rows, axis=0)


def _ada_call(condt, w_ada, b_ada, n_rows):
    depth, d, n6 = w_ada.shape
    return pl.pallas_call(
        functools.partial(_ada_kernel, n_rows=n_rows),
        grid=(depth, n6 // TN_ADA),
        in_specs=[
            pl.BlockSpec((d, SUBLANES), lambda l, j: (0, 0)),
            pl.BlockSpec((1, d, TN_ADA), lambda l, j: (l, 0, j)),
            pl.BlockSpec((1, 1, TN_ADA), lambda l, j: (l, 0, j)),
        ],
        out_specs=pl.BlockSpec((1, SUBLANES, TN_ADA), lambda l, j: (l, 0, j)),
        out_shape=jax.ShapeDtypeStruct((depth, SUBLANES, n6), jnp.float32),
        compiler_params=pltpu.CompilerParams(
            dimension_semantics=("arbitrary", "arbitrary"), vmem_limit_bytes=VMEM_LIMIT_BYTES),
        name="ada_mod",
    )(condt, w_ada, b_ada.reshape(depth, 1, n6))


def _in_proj_kernel(x_ref, mod_ref, w_ref, o_ref, h_scr):
    @pl.when(pl.program_id(1) == 0)
    def _():
        sh = mod_ref[0, 0:1, :]
        sc = mod_ref[0, 1:2, :]
        h_scr[...] = (_layer_norm(x_ref[...]) * (1.0 + sc) + sh).astype(jnp.bfloat16)

    o_ref[...] = jnp.dot(h_scr[...], w_ref[...], preferred_element_type=jnp.float32)


def _in_proj_call(x_all, mod_l, w_in_bf, mod_idx_fn):
    r, d = x_all.shape
    n = w_in_bf.shape[1]
    return pl.pallas_call(
        _in_proj_kernel,
        grid=(r // TM_IN, n // TN_IN),
        in_specs=[
            pl.BlockSpec((TM_IN, d), lambda i, j: (i, 0)),
            pl.BlockSpec((1, 6, d), lambda i, j: (mod_idx_fn(i * TM_IN), 0, 0)),
            pl.BlockSpec((d, TN_IN), lambda i, j: (0, j)),
        ],
        out_specs=pl.BlockSpec((TM_IN, TN_IN), lambda i, j: (i, j)),
        out_shape=jax.ShapeDtypeStruct((r, n), jnp.float32),
        scratch_shapes=[pltpu.VMEM((TM_IN, d), jnp.bfloat16)],
        compiler_params=pltpu.CompilerParams(
            dimension_semantics=("arbitrary", "arbitrary"), vmem_limit_bytes=VMEM_LIMIT_BYTES),
        name="in_proj",
    )(x_all, mod_l, w_in_bf)


def _rope(x, cos, sin_signed, first_or_third):
    rot = jnp.where(first_or_third, pltpu.roll(x, 3 * HEAD_DIM // 4, axis=1), pltpu.roll(x, HEAD_DIM // 4, axis=1))
    return x * cos + rot * sin_signed


def _attn_kernel(sink_ref, q_ref, kp_ref, k0_ref, kn_ref, vp_ref, v0_ref, vn_ref, kc_ref, vc_ref,
                 cq_ref, sq_ref, cp_ref, sp_ref, cn_ref, sn_ref, g_ref, o_ref, *, nb_lat, nb_seq):
    t = pl.program_id(0)
    is_lat = t < nb_lat
    n = t % nb_seq
    has_prev = jnp.logical_and(is_lat, n > 0)
    has_next = jnp.logical_and(is_lat, n < nb_seq - 1)
    scale = HEAD_DIM ** -0.5

    lane = lax.broadcasted_iota(jnp.int32, (BLOCK, HEAD_DIM), 1)
    quarter = lane // (HEAD_DIM // 4)
    fot = jnp.logical_or(quarter == 0, quarter == 2)

    cq, sq = cq_ref[...], sq_ref[...]
    cp, sp = cp_ref[...], sp_ref[...]
    cn, sn = cn_ref[...], sn_ref[...]

    qi = lax.broadcasted_iota(jnp.int32, (BLOCK, 3 * BLOCK), 0)
    kj = lax.broadcasted_iota(jnp.int32, (BLOCK, 3 * BLOCK), 1)
    blk = kj // BLOCK
    rel = kj - BLOCK - qi
    lo_blk = jnp.where(is_lat, jnp.where(has_prev, 0, 1), 3)
    hi_blk = jnp.where(has_next, 2, 1)
    valid = (jnp.abs(rel) <= WINDOW) & (blk >= lo_blk) & (blk <= hi_blk)
    bias = jnp.where(valid, 0.0, NEG_BIG)
    bias4 = jnp.concatenate([bias] * Q_PER_KV, axis=0)

    outs = []
    for g in range(N_KV_HEADS):
        ksl = slice(g * HEAD_DIM, (g + 1) * HEAD_DIM)
        kw = jnp.concatenate([
            _rope(kp_ref[:, ksl], cp, sp, fot),
            _rope(k0_ref[:, ksl], cq, sq, fot),
            _rope(kn_ref[:, ksl], cn, sn, fot)], axis=0).astype(jnp.bfloat16)
        vw = jnp.concatenate([vp_ref[:, ksl], v0_ref[:, ksl], vn_ref[:, ksl]], axis=0).astype(jnp.bfloat16)
        kc = kc_ref[:, ksl].astype(jnp.bfloat16)
        vc = vc_ref[:, ksl].astype(jnp.bfloat16)
        qs = []
        sinks = []
        for r in range(Q_PER_KV):
            h = g * Q_PER_KV + r
            qs.append(_rope(q_ref[:, h * HEAD_DIM:(h + 1) * HEAD_DIM], cq, sq, fot))
            sinks.append(jnp.full((BLOCK, 1), sink_ref[h], jnp.float32))
        qg = jnp.concatenate(qs, axis=0).astype(jnp.bfloat16)
        sink = jnp.concatenate(sinks, axis=0)
        s_w = lax.dot_general(qg, kw, (((1,), (1,)), ((), ())), preferred_element_type=jnp.float32) * scale
        s_c = lax.dot_general(qg, kc, (((1,), (1,)), ((), ())), preferred_element_type=jnp.float32) * scale
        s_w = jnp.where(bias4 == 0.0, s_w, NEG_BIG)
        m = jnp.maximum(jnp.maximum(jnp.max(s_w, axis=-1, keepdims=True), jnp.max(s_c, axis=-1, keepdims=True)), sink)
        p_w = jnp.exp(s_w - m)
        p_c = jnp.exp(s_c - m)
        denom = jnp.sum(p_w, axis=-1, keepdims=True) + jnp.sum(p_c, axis=-1, keepdims=True) + jnp.exp(sink - m)
        o = (jnp.dot(p_w.astype(jnp.bfloat16), vw, preferred_element_type=jnp.float32)
             + jnp.dot(p_c.astype(jnp.bfloat16), vc, preferred_element_type=jnp.float32)) / denom
        o = o * lax.rsqrt(jnp.mean(o * o, axis=-1, keepdims=True) + LN_EPS)
        for r in range(Q_PER_KV):
            h = g * Q_PER_KV + r
            outs.append(o[r * BLOCK:(r + 1) * BLOCK, :] * g_ref[:, h * HEAD_DIM:(h + 1) * HEAD_DIM])
    o_ref[...] = jnp.concatenate(outs, axis=1)


def _attn_call(p, sink_l, g_attn_l, cos_t, sin_t, *, batch, seq, ctx_len, off_q, off_k, off_v):
    r = p.shape[0]
    nb_seq = seq // BLOCK
    nb_lat = batch * nb_seq
    nb_ctx = ctx_len // BLOCK
    attn_w = N_Q_HEADS * HEAD_DIM
    kv_w = N_KV_HEADS * HEAD_DIM
    qb, kb, vb = off_q // attn_w, off_k // kv_w, off_v // kv_w
    last = r // BLOCK - 1

    def prev_i(t):
        return jnp.maximum(t - 1, 0)

    def next_i(t):
        return jnp.minimum(t + 1, last)

    def ctx_i(t):
        b = jnp.where(t < nb_lat, t // nb_seq, (t - nb_lat) // nb_ctx)
        return (batch * seq) // ctx_len + b

    kv_spec = lambda fn, col: pl.BlockSpec((BLOCK, kv_w), lambda t: (fn(t), col))
    cs_spec = lambda fn: pl.BlockSpec((BLOCK, HEAD_DIM), lambda t: (fn(t), 0))
    ident = lambda t: t
    return pl.pallas_call(
        functools.partial(_attn_kernel, nb_lat=nb_lat, nb_seq=nb_seq),
        grid=(r // BLOCK,),
        in_specs=[
            pl.BlockSpec(memory_space=pltpu.SMEM),
            pl.BlockSpec((BLOCK, attn_w), lambda t: (t, qb)),
            kv_spec(prev_i, kb), kv_spec(ident, kb), kv_spec(next_i, kb),
            kv_spec(prev_i, vb), kv_spec(ident, vb), kv_spec(next_i, vb),
            pl.BlockSpec((ctx_len, kv_w), lambda t: (ctx_i(t), kb)),
            pl.BlockSpec((ctx_len, kv_w), lambda t: (ctx_i(t), vb)),
            cs_spec(ident), cs_spec(ident), cs_spec(prev_i), cs_spec(prev_i), cs_spec(next_i), cs_spec(next_i),
            pl.BlockSpec((1, attn_w), lambda t: (0, 0)),
        ],
        out_specs=pl.BlockSpec((BLOCK, attn_w), lambda t: (t, 0)),
        out_shape=jax.ShapeDtypeStruct((r, attn_w), jnp.float32),
        compiler_params=pltpu.CompilerParams(
            dimension_semantics=("arbitrary",), vmem_limit_bytes=VMEM_LIMIT_BYTES),
        name="attn",
    )(sink_l, p, p, p, p, p, p, p, p, p, cos_t, sin_t, cos_t, sin_t, cos_t, sin_t, g_attn_l.reshape(1, attn_w))


def _out_kernel(ya_ref, gb_ref, gc_ref, xt_ref, gcp_ref, xtp_ref, gcn_ref, xtn_ref, cw_ref, gconv_ref,
                w_ref, x_ref, mod_ref, lng_ref, lnb_ref, x1_ref, h2_ref, *, alpha, first_fn, last_fn, conv_w):
    i = pl.program_id(0)
    r0 = i * TM_OUT
    is_first = first_fn(r0)
    is_last = last_fn(r0)
    u = gc_ref[...] * xt_ref[...]
    u_prev_row = jnp.where(is_first, 0.0, gcp_ref[SUBLANES - 1:SUBLANES, :] * xtp_ref[SUBLANES - 1:SUBLANES, :])
    u_next_row = jnp.where(is_last, 0.0, gcn_ref[0:1, :] * xtn_ref[0:1, :])
    row = lax.broadcasted_iota(jnp.int32, u.shape, 0)
    up = jnp.where(row == 0, u_prev_row, pltpu.roll(u, 1, axis=0))
    un = jnp.where(row == TM_OUT - 1, u_next_row, pltpu.roll(u, TM_OUT - 1, axis=0))
    yc = gb_ref[...] * (up * cw_ref[0:1, :] + u * cw_ref[1:2, :] + un * cw_ref[2:3, :])
    parts = []
    for g in range(conv_w // HEAD_DIM):
        blk = yc[:, g * HEAD_DIM:(g + 1) * HEAD_DIM]
        parts.append(blk * lax.rsqrt(jnp.mean(blk * blk, axis=-1, keepdims=True) + LN_EPS))
    ycn = jnp.concatenate(parts, axis=1) * gconv_ref[...]
    cat = jnp.concatenate([ya_ref[...], ycn], axis=1).astype(jnp.bfloat16)
    y = jnp.dot(cat, w_ref[...], preferred_element_type=jnp.float32)
    ga1 = mod_ref[0, 2:3, :]
    sh2 = mod_ref[0, 3:4, :]
    sc2 = mod_ref[0, 4:5, :]
    x1 = _layer_norm(alpha * x_ref[...] + ga1 * y) * lng_ref[...] + lnb_ref[...]
    x1_ref[...] = x1
    h2_ref[...] = _layer_norm(x1) * (1.0 + sc2) + sh2


def _out_call(ya, p, conv_w_l, g_conv_l, w_out_bf, x_all, mod_l, ln_g, ln_b, *, alpha, mod_idx_fn, first_fn, last_fn,
              conv_w):
    r, d = x_all.shape
    nblk8 = r // SUBLANES
    hb = TM_OUT // SUBLANES

    def prev8(i):
        return jnp.maximum(i * hb - 1, 0)

    def next8(i):
        return jnp.minimum((i + 1) * hb, nblk8 - 1)

    tile = lambda col: pl.BlockSpec((TM_OUT, conv_w), lambda i: (i, col))
    halo = lambda fn, col: pl.BlockSpec((SUBLANES, conv_w), lambda i: (fn(i), col))
    vec = lambda w: pl.BlockSpec((1, w), lambda i: (0, 0))
    return pl.pallas_call(
        functools.partial(_out_kernel, alpha=alpha, first_fn=first_fn, last_fn=last_fn, conv_w=conv_w),
        grid=(r // TM_OUT,),
        in_specs=[
            pl.BlockSpec((TM_OUT, ya.shape[1]), lambda i: (i, 0)),
            tile(0), tile(1), tile(2),
            halo(prev8, 1), halo(prev8, 2), halo(next8, 1), halo(next8, 2),
            pl.BlockSpec((CONV_K, conv_w), lambda i: (0, 0)),
            vec(conv_w),
            pl.BlockSpec(w_out_bf.shape, lambda i: (0, 0)),
            pl.BlockSpec((TM_OUT, d), lambda i: (i, 0)),
            pl.BlockSpec((1, 6, d), lambda i: (mod_idx_fn(i * TM_OUT), 0, 0)),
            vec(d), vec(d),
        ],
        out_specs=[pl.BlockSpec((TM_OUT, d), lambda i: (i, 0)), pl.BlockSpec((TM_OUT, d), lambda i: (i, 0))],
        out_shape=[jax.ShapeDtypeStruct((r, d), jnp.float32), jax.ShapeDtypeStruct((r, d), jnp.float32)],
        compiler_params=pltpu.CompilerParams(
            dimension_semantics=("arbitrary",), vmem_limit_bytes=VMEM_LIMIT_BYTES),
        name="conv_out_proj",
    )(ya, p, p, p, p, p, p, p, conv_w_l, g_conv_l.reshape(1, conv_w), w_out_bf, x_all, mod_l,
      ln_g.reshape(1, d), ln_b.reshape(1, d))


def _peer_score_kernel(h_ref, wq_ref, keys_ref, s_ref):
    q = jnp.dot(h_ref[...].astype(jnp.bfloat16), wq_ref[...], preferred_element_type=jnp.float32)
    qb = q.astype(jnp.bfloat16)
    for hp in range(2 * PEER_HEADS):
        s_ref[hp] = lax.dot_general(keys_ref[hp], qb[:, hp * PEER_HALF:(hp + 1) * PEER_HALF],
                                    (((1,), (1,)), ((), ())), preferred_element_type=jnp.float32)


def _peer_score_call(h2, wq_bf, keys_bf):
    r, d = h2.shape
    nq = wq_bf.shape[1]
    return pl.pallas_call(
        _peer_score_kernel,
        grid=(r // TM_PQ,),
        in_specs=[
            pl.BlockSpec((TM_PQ, d), lambda i: (i, 0)),
            pl.BlockSpec((d, nq), lambda i: (0, 0)),
            pl.BlockSpec(keys_bf.shape, lambda i: (0, 0, 0)),
        ],
        out_specs=pl.BlockSpec((2 * PEER_HEADS, N_KEYS, TM_PQ), lambda i: (0, 0, i)),
        out_shape=jax.ShapeDtypeStruct((2 * PEER_HEADS, N_KEYS, r), jnp.float32),
        compiler_params=pltpu.CompilerParams(
            dimension_semantics=("arbitrary",), vmem_limit_bytes=VMEM_LIMIT_BYTES),
        name="peer_scores",
    )(h2, wq_bf, keys_bf)


def _select_max(x, row, sentinel):
    m = jnp.max(x, axis=0, keepdims=True)
    pos = jnp.min(jnp.where(x == m, row, sentinel), axis=0, keepdims=True)
    return m, pos, jnp.where(row == pos, -jnp.inf, x)


def _topk_kernel(s_ref, idx_ref, gate_ref, *, rows_per_expert, row_base):
    tt = s_ref.shape[2]
    k = PEER_TOPK
    key_row = lax.broadcasted_iota(jnp.int32, (N_KEYS, tt), 0)
    k_row = lax.broadcasted_iota(jnp.int32, (k, tt), 0)
    row8 = lax.broadcasted_iota(jnp.int32, (SUBLANES, tt), 0)
    flat_row = jnp.concatenate([row8, row8 + SUBLANES] + [a * k + row8 for a in range(1, SUBLANES)]
                               + [(SUBLANES + row8) * k], axis=0)
    gates, idxs = [], []
    for h in range(PEER_HEADS):
        x1 = s_ref[2 * h]
        x2 = s_ref[2 * h + 1]
        m1s, p1s = [], []
        s1 = jnp.zeros((k, tt), jnp.float32)
        i1 = jnp.zeros((k, tt), jnp.int32)
        s2 = jnp.zeros((k, tt), jnp.float32)
        i2 = jnp.zeros((k, tt), jnp.int32)
        for a in range(k):
            m1, p1, x1 = _select_max(x1, key_row, N_KEYS)
            m2, p2, x2 = _select_max(x2, key_row, N_KEYS)
            m1s.append(m1)
            p1s.append(p1)
            s1 = jnp.where(k_row == a, m1, s1)
            i1 = jnp.where(k_row == a, p1, i1)
            s2 = jnp.where(k_row == a, m2, s2)
            i2 = jnp.where(k_row == a, p2, i2)
        s2_lo, i2_lo = s2[0:SUBLANES], i2[0:SUBLANES]
        vals = [m1s[0] + s2_lo, m1s[0] + s2[SUBLANES:k]]
        ids = [p1s[0] * N_KEYS + i2_lo, p1s[0] * N_KEYS + i2[SUBLANES:k]]
        for a in range(1, SUBLANES):
            vals.append(jnp.where(row8 < k // (a + 1), m1s[a] + s2_lo, -jnp.inf))
            ids.append(p1s[a] * N_KEYS + i2_lo)
        vals.append(s1[SUBLANES:k] + s2[0:1])
        ids.append(i1[SUBLANES:k] * N_KEYS + i2[0:1])
        cand = jnp.concatenate(vals, axis=0)
        cand_i = jnp.concatenate(ids, axis=0)
        top_s = jnp.zeros((k, tt), jnp.float32)
        top_i = jnp.zeros((k, tt), jnp.int32)
        m0 = None
        for it in range(k):
            m, pos, cand_next = _select_max(cand, flat_row, k * k)
            e = jnp.sum(jnp.where(flat_row == pos, cand_i, 0), axis=0, keepdims=True)
            cand = cand_next
            top_s = jnp.where(k_row == it, m, top_s)
            top_i = jnp.where(k_row == it, e, top_i)
            if it == 0:
                m0 = m
        ex = jnp.exp(top_s - m0)
        gates.append(ex / jnp.sum(ex, axis=0, keepdims=True))
        idxs.append(top_i)
    gate_ref[...] = jnp.concatenate(gates, axis=0).T
    idx_ref[...] = (jnp.concatenate(idxs, axis=0) * rows_per_expert + row_base).T


def _topk_call(s, rows_per_expert=1, row_base=0):
    r = s.shape[2]
    w = PEER_HEADS * PEER_TOPK
    return pl.pallas_call(
        functools.partial(_topk_kernel, rows_per_expert=rows_per_expert, row_base=row_base),
        grid=(r // TK_TOPK,),
        in_specs=[pl.BlockSpec((2 * PEER_HEADS, N_KEYS, TK_TOPK), lambda i: (0, 0, i))],
        out_specs=[pl.BlockSpec((TK_TOPK, w), lambda i: (i, 0)), pl.BlockSpec((TK_TOPK, w), lambda i: (i, 0))],
        out_shape=[jax.ShapeDtypeStruct((r, w), jnp.int32), jax.ShapeDtypeStruct((r, w), jnp.float32)],
        compiler_params=pltpu.CompilerParams(
            dimension_semantics=("arbitrary",), vmem_limit_bytes=VMEM_LIMIT_BYTES),
        name="peer_topk",
    )(s)


def _peer_start_token(idx_ref, row, uv_hbm, buf, sem, j, k_lo=0, k_hi=PEER_HEADS * PEER_TOPK):
    nsel = PEER_HEADS * PEER_TOPK
    nch = buf.shape[0] // (TB_PEER * nsel // SUBLANES)
    for k in range(k_lo, k_hi):
        first = pl.multiple_of(idx_ref[row, k], nch)
        group = (j * nsel + k) // SUBLANES
        pltpu.make_async_copy(uv_hbm.at[pl.ds(first, nch)],
                              buf.at[pl.ds(group * nch, nch), pl.ds(k % SUBLANES, 1), :], sem).start(priority=k % 2)


def _peer_wait_block(uv_hbm, buf, sem):
    for _ in range(SUBLANES):
        pltpu.make_async_copy(uv_hbm.at[pl.ds(0, buf.shape[0])], buf.at[:, pl.ds(0, 1), :], sem).wait()


def _peer_token(buf, j, h_row, gate_col, start_some):
    nsel = PEER_HEADS * PEER_TOPK
    ngrp = nsel // SUBLANES
    nch = h_row.shape[1] // LANES
    h3 = h_row.reshape(nch, 1, LANES)
    a_parts = []
    for g in range(ngrp):
        start_some(g, 2 * ngrp)
        w = buf[pl.ds((j * ngrp + g) * nch, nch)]
        u = lax.bitcast_convert_type(w & jnp.uint32(0xFFFF0000), jnp.float32)
        a_parts.append(jnp.sum(jnp.sum(u * h3, axis=0), axis=-1, keepdims=True))
    a = jnp.concatenate(a_parts, axis=0)
    act = 0.5 * a * (1.0 + lax.erf(a * (1.0 / math.sqrt(2.0))))
    wgt = gate_col * act
    out = jnp.zeros((nch, SUBLANES, LANES), jnp.float32)
    for g in range(ngrp):
        start_some(ngrp + g, 2 * ngrp)
        w = buf[pl.ds((j * ngrp + g) * nch, nch)]
        v = lax.bitcast_convert_type(w << 16, jnp.float32)
        out = out + v * wgt[g * SUBLANES:(g + 1) * SUBLANES][None]
    return jnp.sum(out, axis=1, keepdims=True).reshape(1, nch * LANES)


def _peer_kernel(idx_ref, idx_next_ref, gate_ref, h_ref, x_ref, mod_ref, lng_ref, lnb_ref, uv_hbm, o_ref,
                 *scratch, alpha):
    bufs, y_scr, sems = scratch[:NBUF_PEER], scratch[NBUF_PEER], scratch[NBUF_PEER + 1]
    s = pl.program_id(0)
    tb = TB_PEER
    ahead = NBUF_PEER - 1

    @pl.when(s == 0)
    def _():
        for p in range(ahead):
            for j in range(tb):
                _peer_start_token(idx_ref, p * tb + j, uv_hbm, bufs[p], sems.at[p], j)

    gate_t = gate_ref[...].T
    for p in range(NBUF_PEER):
        _peer_wait_block(uv_hbm, bufs[p], sems.at[p])
        q = p + ahead
        dst = q % NBUF_PEER
        src_idx, src_row0 = (idx_ref, q * tb) if q < NBUF_PEER else (idx_next_ref, (q - NBUF_PEER) * tb)
        nsel = PEER_HEADS * PEER_TOPK
        for j in range(tb):
            def start_some(i, n, j=j):
                _peer_start_token(src_idx, src_row0 + j, uv_hbm, bufs[dst], sems.at[dst], j,
                                  i * nsel // n, (i + 1) * nsel // n)
            row = p * tb + j
            y_scr[row:row + 1, :] = _peer_token(bufs[p], j, h_ref[row:row + 1, :], gate_t[:, row:row + 1], start_some)

    @pl.when(s == pl.num_programs(0) - 1)
    def _():
        for p in range(ahead):
            _peer_wait_block(uv_hbm, bufs[p], sems.at[p])

    ga2 = mod_ref[0, 5:6, :]
    o_ref[...] = _layer_norm(alpha * x_ref[...] + ga2 * y_scr[...]) * lng_ref[...] + lnb_ref[...]


def _peer_call(idx, gate, h2, x1, mod_l, ln_g, ln_b, uv_tab, *, alpha, mod_idx_fn):
    r, d = h2.shape
    nsel = PEER_HEADS * PEER_TOPK
    rows = NBUF_PEER * TB_PEER
    nstep = r // rows
    nxt = lambda s: jnp.minimum(s + 1, nstep - 1)
    return pl.pallas_call(
        functools.partial(_peer_kernel, alpha=alpha),
        grid=(nstep,),
        in_specs=[
            pl.BlockSpec((rows, nsel), lambda s: (s, 0), memory_space=pltpu.SMEM),
            pl.BlockSpec((rows, nsel), lambda s: (nxt(s), 0), memory_space=pltpu.SMEM),
            pl.BlockSpec((rows, nsel), lambda s: (s, 0)),
            pl.BlockSpec((rows, d), lambda s: (s, 0)),
            pl.BlockSpec((rows, d), lambda s: (s, 0)),
            pl.BlockSpec((1, 6, d), lambda s: (mod_idx_fn(s * rows), 0, 0)),
            pl.BlockSpec((1, d), lambda s: (0, 0)),
            pl.BlockSpec((1, d), lambda s: (0, 0)),
            pl.BlockSpec(memory_space=pl.ANY),
        ],
        out_specs=pl.BlockSpec((rows, d), lambda s: (s, 0)),
        out_shape=jax.ShapeDtypeStruct((r, d), jnp.float32),
        scratch_shapes=[
            *[pltpu.VMEM((TB_PEER * nsel // SUBLANES * (d // LANES), SUBLANES, LANES), jnp.uint32)
              for _ in range(NBUF_PEER)],
            pltpu.VMEM((rows, d), jnp.float32),
            pltpu.SemaphoreType.DMA((NBUF_PEER,)),
        ],
        compiler_params=pltpu.CompilerParams(
            dimension_semantics=("arbitrary",), vmem_limit_bytes=VMEM_LIMIT_BYTES),
        name="peer_experts",
    )(idx, idx, gate, h2, x1, mod_l, ln_g.reshape(1, d), ln_b.reshape(1, d), uv_tab)


def _rope_tables(batch, seq, n_ctx_rows):
    rows = seq // GRID_W
    row = jnp.broadcast_to(jnp.arange(rows)[:, None], (rows, GRID_W)).reshape(-1)
    col = jnp.broadcast_to(jnp.arange(GRID_W)[None, :], (rows, GRID_W)).reshape(-1)
    inv = ROPE_BASE ** (-jnp.arange(0, ROT_HALF, 2, dtype=jnp.float32) / ROT_HALF)
    ar = row.astype(jnp.float32)[:, None] * inv
    ac = col.astype(jnp.float32)[:, None] * inv
    ang = jnp.concatenate([ar, ar, ac, ac], axis=-1)
    cos, sin = jnp.cos(ang), jnp.sin(ang)
    quarter = jnp.arange(HEAD_DIM) // (HEAD_DIM // 4)
    sign = jnp.where((quarter == 0) | (quarter == 2), -1.0, 1.0).astype(jnp.float32)
    sin_signed = sin * sign
    cos_t = jnp.concatenate([jnp.tile(cos, (batch, 1)), jnp.ones((n_ctx_rows, HEAD_DIM), jnp.float32)], axis=0)
    sin_t = jnp.concatenate([jnp.tile(sin_signed, (batch, 1)), jnp.zeros((n_ctx_rows, HEAD_DIM), jnp.float32)], axis=0)
    return cos_t, sin_t


def kernel(x, c, ctx, c_ctx, w_ada, b_ada, w_in, conv_w, attn_sink, g_attn, g_conv, w_out, ln1_g, ln1_b,
           peer_wq, peer_keys, peer_u, peer_v, ln2_g, ln2_b):
    batch, seq, d = x.shape
    ctx_len = ctx.shape[1]
    depth = w_ada.shape[0]
    conv_width = conv_w.shape[2]
    attn_w = N_Q_HEADS * HEAD_DIM
    kv_w = N_KV_HEADS * HEAD_DIM
    off_q = 3 * conv_width
    off_k = off_q + attn_w
    off_v = off_k + kv_w
    n_lat = batch * seq
    r = n_lat + batch * ctx_len
    alpha = (2.0 * depth) ** 0.25
    assert batch + 1 <= SUBLANES and seq % TM_IN == 0 and (batch * ctx_len) % TM_IN == 0
    assert ctx_len == TM_OUT and seq % TM_OUT == 0 and ctx_len % BLOCK == 0 and seq % GRID_W == 0
    assert r % (NBUF_PEER * TB_PEER) == 0 and seq % (NBUF_PEER * TB_PEER) == 0 and r % TK_TOPK == 0
    assert w_in.shape[2] % TN_IN == 0 and w_ada.shape[2] % TN_ADA == 0 and r % TM_PQ == 0

    def mod_idx_fn(row0):
        return jnp.where(row0 < n_lat, row0 // seq, batch)

    def first_fn(row0):
        return jnp.where(row0 < n_lat, row0 % seq == 0, (row0 - n_lat) % ctx_len == 0)

    def last_fn(row0):
        end = row0 + TM_OUT
        return jnp.where(row0 < n_lat, end % seq == 0, (end - n_lat) % ctx_len == 0)

    x_all = jnp.concatenate([x.reshape(n_lat, d), ctx.reshape(batch * ctx_len, d)], axis=0)
    cond_rows = jnp.concatenate([c, c_ctx[None, :], jnp.zeros((SUBLANES - batch - 1, d), jnp.float32)], axis=0)
    mod = _ada_call(cond_rows.T, w_ada, b_ada, batch + 1).reshape(depth, SUBLANES, 6, d)
    cos_t, sin_t = _rope_tables(batch, seq, batch * ctx_len)
    u_bits = lax.bitcast_convert_type(peer_u.astype(jnp.bfloat16), jnp.uint16).astype(jnp.uint32)
    v_bits = lax.bitcast_convert_type(peer_v.astype(jnp.bfloat16), jnp.uint16).astype(jnp.uint32)
    n_experts = peer_u.shape[1]
    nch = d // LANES
    uv_tab = ((u_bits << 16) | v_bits).reshape(depth * n_experts * nch, 1, LANES)

    for layer in range(depth):
        mod_l = mod[layer]
        p = _in_proj_call(x_all, mod_l, w_in[layer].astype(jnp.bfloat16), mod_idx_fn)
        ya = _attn_call(p, attn_sink[layer], g_attn[layer], cos_t, sin_t, batch=batch, seq=seq, ctx_len=ctx_len,
                        off_q=off_q, off_k=off_k, off_v=off_v)
        x1, h2 = _out_call(ya, p, conv_w[layer], g_conv[layer], w_out[layer].astype(jnp.bfloat16), x_all, mod_l,
                           ln1_g[layer], ln1_b[layer], alpha=alpha, mod_idx_fn=mod_idx_fn, first_fn=first_fn,
                           last_fn=last_fn, conv_w=conv_width)
        keys_bf = peer_keys[layer].reshape(2 * PEER_HEADS, N_KEYS, PEER_HALF).astype(jnp.bfloat16)
        s = _peer_score_call(h2, peer_wq[layer].astype(jnp.bfloat16), keys_bf)
        idx, gate = _topk_call(s, rows_per_expert=nch, row_base=layer * n_experts * nch)
        x_all = _peer_call(idx, gate, h2, x1, mod_l, ln2_g[layer], ln2_b[layer], uv_tab,
                           alpha=alpha, mod_idx_fn=mod_idx_fn)
    return x_all[:n_lat].reshape(batch, seq, d)
```

```python
import functools
import math

import jax
import jax.numpy as jnp
from jax import lax
from jax.experimental import pallas as pl
from jax.experimental.pallas import tpu as pltpu

HEAD_DIM = 128
N_Q_HEADS = 8
N_KV_HEADS = 2
Q_PER_KV = N_Q_HEADS // N_KV_HEADS
GRID_W = 64
WINDOW = 128
BLOCK = 128
ROPE_BASE = 10000.0
ROT_HALF = HEAD_DIM // 2
CONV_K = 3
PEER_HEADS = 8
N_KEYS = 128
PEER_TOPK = 16
PEER_HALF = 128
LN_EPS = 1e-6
NEG_BIG = -1e30

LANES = 128
SUBLANES = 8
VMEM_LIMIT_BYTES = 56 * 1024 * 1024

TM_IN = 512
TN_IN = 1536
TM_OUT = 256
TM_PQ = 256
TK_TOPK = 128
TB_PEER = 8
NBUF_PEER = 4
TN_ADA = 512


def _layer_norm(x):
    mu = jnp.mean(x, axis=-1, keepdims=True)
    xc = x - mu
    var = jnp.mean(xc * xc, axis=-1, keepdims=True)
    return xc * lax.rsqrt(var + LN_EPS)


def _ada_kernel(condt_ref, w_ref, b_ref, o_ref, *, n_rows):
    ct = condt_ref[...]
    ct = ct * (1.0 / (1.0 + jnp.exp(-ct)))
    w = w_ref[0]
    rows = []
    for r in range(SUBLANES):
        if r < n_rows:
            rows.append(jnp.sum(w * ct[:, r:r + 1], axis=0, keepdims=True) + b_ref[0])
        else:
            rows.append(jnp.zeros_like(b_ref[0]))
    o_ref[0] = jnp.concatenate(rows, axis=0)


def _ada_call(condt, w_ada, b_ada, n_rows):
    depth, d, n6 = w_ada.shape
    return pl.pallas_call(
        functools.partial(_ada_kernel, n_rows=n_rows),
        grid=(depth, n6 // TN_ADA),
        in_specs=[
            pl.BlockSpec((d, SUBLANES), lambda l, j: (0, 0)),
            pl.BlockSpec((1, d, TN_ADA), lambda l, j: (l, 0, j)),
            pl.BlockSpec((1, 1, TN_ADA), lambda l, j: (l, 0, j)),
        ],
        out_specs=pl.BlockSpec((1, SUBLANES, TN_ADA), lambda l, j: (l, 0, j)),
        out_shape=jax.ShapeDtypeStruct((depth, SUBLANES, n6), jnp.float32),
        compiler_params=pltpu.CompilerParams(
            dimension_semantics=("arbitrary", "arbitrary"), vmem_limit_bytes=VMEM_LIMIT_BYTES),
        name="ada_mod",
    )(condt, w_ada, b_ada.reshape(depth, 1, n6))


def _in_proj_kernel(x_ref, mod_ref, w_ref, o_ref, h_scr):
    @pl.when(pl.program_id(1) == 0)
    def _():
        sh = mod_ref[0, 0:1, :]
        sc = mod_ref[0, 1:2, :]
        h_scr[...] = (_layer_norm(x_ref[...]) * (1.0 + sc) + sh).astype(jnp.bfloat16)

    o_ref[...] = jnp.dot(h_scr[...], w_ref[...], preferred_element_type=jnp.float32)


def _in_proj_call(x_all, mod_l, w_in_bf, mod_idx_fn):
    r, d = x_all.shape
    n = w_in_bf.shape[1]
    return pl.pallas_call(
        _in_proj_kernel,
        grid=(r // TM_IN, n // TN_IN),
        in_specs=[
            pl.BlockSpec((TM_IN, d), lambda i, j: (i, 0)),
            pl.BlockSpec((1, 6, d), lambda i, j: (mod_idx_fn(i * TM_IN), 0, 0)),
            pl.BlockSpec((d, TN_IN), lambda i, j: (0, j)),
        ],
        out_specs=pl.BlockSpec((TM_IN, TN_IN), lambda i, j: (i, j)),
        out_shape=jax.ShapeDtypeStruct((r, n), jnp.float32),
        scratch_shapes=[pltpu.VMEM((TM_IN, d), jnp.bfloat16)],
        compiler_params=pltpu.CompilerParams(
            dimension_semantics=("arbitrary", "arbitrary"), vmem_limit_bytes=VMEM_LIMIT_BYTES),
        name="in_proj",
    )(x_all, mod_l, w_in_bf)


def _rope(x, cos, sin_signed, first_or_third):
    rot = jnp.where(first_or_third, pltpu.roll(x, 3 * HEAD_DIM // 4, axis=1), pltpu.roll(x, HEAD_DIM // 4, axis=1))
    return x * cos + rot * sin_signed


def _attn_kernel(sink_ref, q_ref, kp_ref, k0_ref, kn_ref, vp_ref, v0_ref, vn_ref, kc_ref, vc_ref,
                 cq_ref, sq_ref, cp_ref, sp_ref, cn_ref, sn_ref, g_ref, o_ref, *, nb_lat, nb_seq):
    t = pl.program_id(0)
    is_lat = t < nb_lat
    n = t % nb_seq
    has_prev = jnp.logical_and(is_lat, n > 0)
    has_next = jnp.logical_and(is_lat, n < nb_seq - 1)
    scale = HEAD_DIM ** -0.5

    lane = lax.broadcasted_iota(jnp.int32, (BLOCK, HEAD_DIM), 1)
    quarter = lane // (HEAD_DIM // 4)
    fot = jnp.logical_or(quarter == 0, quarter == 2)

    cq, sq = cq_ref[...], sq_ref[...]
    cp, sp = cp_ref[...], sp_ref[...]
    cn, sn = cn_ref[...], sn_ref[...]

    qi = lax.broadcasted_iota(jnp.int32, (BLOCK, 3 * BLOCK), 0)
    kj = lax.broadcasted_iota(jnp.int32, (BLOCK, 3 * BLOCK), 1)
    blk = kj // BLOCK
    rel = kj - BLOCK - qi
    lo_blk = jnp.where(is_lat, jnp.where(has_prev, 0, 1), 3)
    hi_blk = jnp.where(has_next, 2, 1)
    valid = (jnp.abs(rel) <= WINDOW) & (blk >= lo_blk) & (blk <= hi_blk)
    bias = jnp.where(valid, 0.0, NEG_BIG)
    bias4 = jnp.concatenate([bias] * Q_PER_KV, axis=0)

    outs = []
    for g in range(N_KV_HEADS):
        ksl = slice(g * HEAD_DIM, (g + 1) * HEAD_DIM)
        kw = jnp.concatenate([
            _rope(kp_ref[:, ksl], cp, sp, fot),
            _rope(k0_ref[:, ksl], cq, sq, fot),
            _rope(kn_ref[:, ksl], cn, sn, fot)], axis=0).astype(jnp.bfloat16)
        vw = jnp.concatenate([vp_ref[:, ksl], v0_ref[:, ksl], vn_ref[:, ksl]], axis=0).astype(jnp.bfloat16)
        kc = kc_ref[:, ksl].astype(jnp.bfloat16)
        vc = vc_ref[:, ksl].astype(jnp.bfloat16)
        qs = []
        sinks = []
        for r in range(Q_PER_KV):
            h = g * Q_PER_KV + r
            qs.append(_rope(q_ref[:, h * HEAD_DIM:(h + 1) * HEAD_DIM], cq, sq, fot))
            sinks.append(jnp.full((BLOCK, 1), sink_ref[h], jnp.float32))
        qg = jnp.concatenate(qs, axis=0).astype(jnp.bfloat16)
        sink = jnp.concatenate(sinks, axis=0)
        s_w = lax.dot_general(qg, kw, (((1,), (1,)), ((), ())), preferred_element_type=jnp.float32) * scale
        s_c = lax.dot_general(qg, kc, (((1,), (1,)), ((), ())), preferred_element_type=jnp.float32) * scale
        s_w = jnp.where(bias4 == 0.0, s_w, NEG_BIG)
        m = jnp.maximum(jnp.maximum(jnp.max(s_w, axis=-1, keepdims=True), jnp.max(s_c, axis=-1, keepdims=True)), sink)
        p_w = jnp.exp(s_w - m)
        p_c = jnp.exp(s_c - m)
        denom = jnp.sum(p_w, axis=-1, keepdims=True) + jnp.sum(p_c, axis=-1, keepdims=True) + jnp.exp(sink - m)
        o = (jnp.dot(p_w.astype(jnp.bfloat16), vw, preferred_element_type=jnp.float32)
             + jnp.dot(p_c.astype(jnp.bfloat16), vc, preferred_element_type=jnp.float32)) / denom
        o = o * lax.rsqrt(jnp.mean(o * o, axis=-1, keepdims=True) + LN_EPS)
        for r in range(Q_PER_KV):
            h = g * Q_PER_KV + r
            outs.append(o[r * BLOCK:(r + 1) * BLOCK, :] * g_ref[:, h * HEAD_DIM:(h + 1) * HEAD_DIM])
    o_ref[...] = jnp.concatenate(outs, axis=1)


def _attn_call(p, sink_l, g_attn_l, cos_t, sin_t, *, batch, seq, ctx_len, off_q, off_k, off_v):
    r = p.shape[0]
    nb_seq = seq // BLOCK
    nb_lat = batch * nb_seq
    nb_ctx = ctx_len // BLOCK
    attn_w = N_Q_HEADS * HEAD_DIM
    kv_w = N_KV_HEADS * HEAD_DIM
    qb, kb, vb = off_q // attn_w, off_k // kv_w, off_v // kv_w
    last = r // BLOCK - 1

    def prev_i(t):
        return jnp.maximum(t - 1, 0)

    def next_i(t):
        return jnp.minimum(t + 1, last)

    def ctx_i(t):
        b = jnp.where(t < nb_lat, t // nb_seq, (t - nb_lat) // nb_ctx)
        return (batch * seq) // ctx_len + b

    kv_spec = lambda fn, col: pl.BlockSpec((BLOCK, kv_w), lambda t: (fn(t), col))
    cs_spec = lambda fn: pl.BlockSpec((BLOCK, HEAD_DIM), lambda t: (fn(t), 0))
    ident = lambda t: t
    return pl.pallas_call(
        functools.partial(_attn_kernel, nb_lat=nb_lat, nb_seq=nb_seq),
        grid=(r // BLOCK,),
        in_specs=[
            pl.BlockSpec(memory_space=pltpu.SMEM),
            pl.BlockSpec((BLOCK, attn_w), lambda t: (t, qb)),
            kv_spec(prev_i, kb), kv_spec(ident, kb), kv_spec(next_i, kb),
            kv_spec(prev_i, vb), kv_spec(ident, vb), kv_spec(next_i, vb),
            pl.BlockSpec((ctx_len, kv_w), lambda t: (ctx_i(t), kb)),
            pl.BlockSpec((ctx_len, kv_w), lambda t: (ctx_i(t), vb)),
            cs_spec(ident), cs_spec(ident), cs_spec(prev_i), cs_spec(prev_i), cs_spec(next_i), cs_spec(next_i),
            pl.BlockSpec((1, attn_w), lambda t: (0, 0)),
        ],
        out_specs=pl.BlockSpec((BLOCK, attn_w), lambda t: (t, 0)),
        out_shape=jax.ShapeDtypeStruct((r, attn_w), jnp.float32),
        compiler_params=pltpu.CompilerParams(
            dimension_semantics=("arbitrary",), vmem_limit_bytes=VMEM_LIMIT_BYTES),
        name="attn",
    )(sink_l, p, p, p, p, p, p, p, p, p, cos_t, sin_t, cos_t, sin_t, cos_t, sin_t, g_attn_l.reshape(1, attn_w))


def _out_kernel(ya_ref, gb_ref, gc_ref, xt_ref, gcp_ref, xtp_ref, gcn_ref, xtn_ref, cw_ref, gconv_ref,
                w_ref, x_ref, mod_ref, lng_ref, lnb_ref, x1_ref, h2_ref, *, alpha, first_fn, last_fn, conv_w):
    i = pl.program_id(0)
    r0 = i * TM_OUT
    is_first = first_fn(r0)
    is_last = last_fn(r0)
    u = gc_ref[...] * xt_ref[...]
    u_prev_row = jnp.where(is_first, 0.0, gcp_ref[SUBLANES - 1:SUBLANES, :] * xtp_ref[SUBLANES - 1:SUBLANES, :])
    u_next_row = jnp.where(is_last, 0.0, gcn_ref[0:1, :] * xtn_ref[0:1, :])
    row = lax.broadcasted_iota(jnp.int32, u.shape, 0)
    up = jnp.where(row == 0, u_prev_row, pltpu.roll(u, 1, axis=0))
    un = jnp.where(row == TM_OUT - 1, u_next_row, pltpu.roll(u, TM_OUT - 1, axis=0))
    yc = gb_ref[...] * (up * cw_ref[0:1, :] + u * cw_ref[1:2, :] + un * cw_ref[2:3, :])
    parts = []
    for g in range(conv_w // HEAD_DIM):
        blk = yc[:, g * HEAD_DIM:(g + 1) * HEAD_DIM]
        parts.append(blk * lax.rsqrt(jnp.mean(blk * blk, axis=-1, keepdims=True) + LN_EPS))
    ycn = jnp.concatenate(parts, axis=1) * gconv_ref[...]
    cat = jnp.concatenate([ya_ref[...], ycn], axis=1).astype(jnp.bfloat16)
    y = jnp.dot(cat, w_ref[...], preferred_element_type=jnp.float32)
    ga1 = mod_ref[0, 2:3, :]
    sh2 = mod_ref[0, 3:4, :]
    sc2 = mod_ref[0, 4:5, :]
    x1 = _layer_norm(alpha * x_ref[...] + ga1 * y) * lng_ref[...] + lnb_ref[...]
    x1_ref[...] = x1
    h2_ref[...] = _layer_norm(x1) * (1.0 + sc2) + sh2


def _out_call(ya, p, conv_w_l, g_conv_l, w_out_bf, x_all, mod_l, ln_g, ln_b, *, alpha, mod_idx_fn, first_fn, last_fn,
              conv_w):
    r, d = x_all.shape
    nblk8 = r // SUBLANES
    hb = TM_OUT // SUBLANES

    def prev8(i):
        return jnp.maximum(i * hb - 1, 0)

    def next8(i):
        return jnp.minimum((i + 1) * hb, nblk8 - 1)

    tile = lambda col: pl.BlockSpec((TM_OUT, conv_w), lambda i: (i, col))
    halo = lambda fn, col: pl.BlockSpec((SUBLANES, conv_w), lambda i: (fn(i), col))
    vec = lambda w: pl.BlockSpec((1, w), lambda i: (0, 0))
    return pl.pallas_call(
        functools.partial(_out_kernel, alpha=alpha, first_fn=first_fn, last_fn=last_fn, conv_w=conv_w),
        grid=(r // TM_OUT,),
        in_specs=[
            pl.BlockSpec((TM_OUT, ya.shape[1]), lambda i: (i, 0)),
            tile(0), tile(1), tile(2),
            halo(prev8, 1), halo(prev8, 2), halo(next8, 1), halo(next8, 2),
            pl.BlockSpec((CONV_K, conv_w), lambda i: (0, 0)),
            vec(conv_w),
            pl.BlockSpec(w_out_bf.shape, lambda i: (0, 0)),
            pl.BlockSpec((TM_OUT, d), lambda i: (i, 0)),
            pl.BlockSpec((1, 6, d), lambda i: (mod_idx_fn(i * TM_OUT), 0, 0)),
            vec(d), vec(d),
        ],
        out_specs=[pl.BlockSpec((TM_OUT, d), lambda i: (i, 0)), pl.BlockSpec((TM_OUT, d), lambda i: (i, 0))],
        out_shape=[jax.ShapeDtypeStruct((r, d), jnp.float32), jax.ShapeDtypeStruct((r, d), jnp.float32)],
        compiler_params=pltpu.CompilerParams(
            dimension_semantics=("arbitrary",), vmem_limit_bytes=VMEM_LIMIT_BYTES),
        name="conv_out_proj",
    )(ya, p, p, p, p, p, p, p, conv_w_l, g_conv_l.reshape(1, conv_w), w_out_bf, x_all, mod_l,
      ln_g.reshape(1, d), ln_b.reshape(1, d))


def _peer_score_kernel(h_ref, wq_ref, keys_ref, s_ref):
    q = jnp.dot(h_ref[...].astype(jnp.bfloat16), wq_ref[...], preferred_element_type=jnp.float32)
    qb = q.astype(jnp.bfloat16)
    for hp in range(2 * PEER_HEADS):
        s_ref[hp] = lax.dot_general(keys_ref[hp], qb[:, hp * PEER_HALF:(hp + 1) * PEER_HALF],
                                    (((1,), (1,)), ((), ())), preferred_element_type=jnp.float32)


def _peer_score_call(h2, wq_bf, keys_bf):
    r, d = h2.shape
    nq = wq_bf.shape[1]
    return pl.pallas_call(
        _peer_score_kernel,
        grid=(r // TM_PQ,),
        in_specs=[
            pl.BlockSpec((TM_PQ, d), lambda i: (i, 0)),
            pl.BlockSpec((d, nq), lambda i: (0, 0)),
            pl.BlockSpec(keys_bf.shape, lambda i: (0, 0, 0)),
        ],
        out_specs=pl.BlockSpec((2 * PEER_HEADS, N_KEYS, TM_PQ), lambda i: (0, 0, i)),
        out_shape=jax.ShapeDtypeStruct((2 * PEER_HEADS, N_KEYS, r), jnp.float32),
        compiler_params=pltpu.CompilerParams(
            dimension_semantics=("arbitrary",), vmem_limit_bytes=VMEM_LIMIT_BYTES),
        name="peer_scores",
    )(h2, wq_bf, keys_bf)


def _select_max(x, row, sentinel):
    m = jnp.max(x, axis=0, keepdims=True)
    pos = jnp.min(jnp.where(x == m, row, sentinel), axis=0, keepdims=True)
    return m, pos, jnp.where(row == pos, -jnp.inf, x)


def _topk_kernel(s_ref, idx_ref, gate_ref, *, rows_per_expert, row_base):
    tt = s_ref.shape[2]
    k = PEER_TOPK
    key_row = lax.broadcasted_iota(jnp.int32, (N_KEYS, tt), 0)
    k_row = lax.broadcasted_iota(jnp.int32, (k, tt), 0)
    row8 = lax.broadcasted_iota(jnp.int32, (SUBLANES, tt), 0)
    flat_row = jnp.concatenate([row8, row8 + SUBLANES] + [a * k + row8 for a in range(1, SUBLANES)]
                               + [(SUBLANES + row8) * k], axis=0)
    gates, idxs = [], []
    for h in range(PEER_HEADS):
        x1 = s_ref[2 * h]
        x2 = s_ref[2 * h + 1]
        m1s, p1s = [], []
        s1 = jnp.zeros((k, tt), jnp.float32)
        i1 = jnp.zeros((k, tt), jnp.int32)
        s2 = jnp.zeros((k, tt), jnp.float32)
        i2 = jnp.zeros((k, tt), jnp.int32)
        for a in range(k):
            m1, p1, x1 = _select_max(x1, key_row, N_KEYS)
            m2, p2, x2 = _select_max(x2, key_row, N_KEYS)
            m1s.append(m1)
            p1s.append(p1)
            s1 = jnp.where(k_row == a, m1, s1)
            i1 = jnp.where(k_row == a, p1, i1)
            s2 = jnp.where(k_row == a, m2, s2)
            i2 = jnp.where(k_row == a, p2, i2)
        s2_lo, i2_lo = s2[0:SUBLANES], i2[0:SUBLANES]
        vals = [m1s[0] + s2_lo, m1s[0] + s2[SUBLANES:k]]
        ids = [p1s[0] * N_KEYS + i2_lo, p1s[0] * N_KEYS + i2[SUBLANES:k]]
        for a in range(1, SUBLANES):
            vals.append(jnp.where(row8 < k // (a + 1), m1s[a] + s2_lo, -jnp.inf))
            ids.append(p1s[a] * N_KEYS + i2_lo)
        vals.append(s1[SUBLANES:k] + s2[0:1])
        ids.append(i1[SUBLANES:k] * N_KEYS + i2[0:1])
        cand = jnp.concatenate(vals, axis=0)
        cand_i = jnp.concatenate(ids, axis=0)
        top_s = jnp.zeros((k, tt), jnp.float32)
        top_i = jnp.zeros((k, tt), jnp.int32)
        m0 = None
        for it in range(k):
            m, pos, cand_next = _select_max(cand, flat_row, k * k)
            e = jnp.sum(jnp.where(flat_row == pos, cand_i, 0), axis=0, keepdims=True)
            cand = cand_next
            top_s = jnp.where(k_row == it, m, top_s)
            top_i = jnp.where(k_row == it, e, top_i)
            if it == 0:
                m0 = m
        ex = jnp.exp(top_s - m0)
        gates.append(ex / jnp.sum(ex, axis=0, keepdims=True))
        idxs.append(top_i)
    gate_ref[...] = jnp.concatenate(gates, axis=0).T
    idx_ref[...] = (jnp.concatenate(idxs, axis=0) * rows_per_expert + row_base).T


def _topk_call(s, rows_per_expert=1, row_base=0):
    r = s.shape[2]
    w = PEER_HEADS * PEER_TOPK
    return pl.pallas_call(
        functools.partial(_topk_kernel, rows_per_expert=rows_per_expert, row_base=row_base),
        grid=(r // TK_TOPK,),
        in_specs=[pl.BlockSpec((2 * PEER_HEADS, N_KEYS, TK_TOPK), lambda i: (0, 0, i))],
        out_specs=[pl.BlockSpec((TK_TOPK, w), lambda i: (i, 0)), pl.BlockSpec((TK_TOPK, w), lambda i: (i, 0))],
        out_shape=[jax.ShapeDtypeStruct((r, w), jnp.int32), jax.ShapeDtypeStruct((r, w), jnp.float32)],
        compiler_params=pltpu.CompilerParams(
            dimension_semantics=("arbitrary",), vmem_limit_bytes=VMEM_LIMIT_BYTES),
        name="peer_topk",
    )(s)


def _peer_start_token(idx_ref, row, uv_hbm, buf, sem, j):
    nsel = PEER_HEADS * PEER_TOPK
    tpe = buf.shape[0] // (TB_PEER * nsel)
    for k in range(nsel):
        first = pl.multiple_of(idx_ref[row, k], tpe)
        pltpu.make_async_copy(uv_hbm.at[pl.ds(first, tpe)], buf.at[pl.ds((j * nsel + k) * tpe, tpe)],
                              sem).start(priority=k % 2)


def _peer_wait_block(uv_hbm, buf, sem):
    pltpu.make_async_copy(uv_hbm.at[pl.ds(0, buf.shape[0])], buf, sem).wait()


def _sum_sublanes_of_8(ps):
    sub = lax.broadcasted_iota(jnp.int32, (SUBLANES, LANES), 0)
    dist = SUBLANES // 2
    while len(ps) > 1:
        half = len(ps) // 2
        low = (sub % (2 * dist)) < dist
        ps = [jnp.where(low, ps[i], ps[i + half])
              + jnp.where(low, pltpu.roll(ps[i], SUBLANES - dist, axis=0), pltpu.roll(ps[i + half], dist, axis=0))
              for i in range(half)]
        dist //= 2
    return ps[0]


def _peer_token(buf, j, h_tiles, gate_col, w_scr):
    nsel = PEER_HEADS * PEER_TOPK
    tpe = len(h_tiles)
    hi_mask = jnp.uint32(0xFFFF0000)
    a_parts = []
    for g in range(nsel // SUBLANES):
        ps = []
        for i in range(SUBLANES):
            base = (j * nsel + g * SUBLANES + i) * tpe
            p = None
            for t in range(tpe):
                term = lax.bitcast_convert_type(buf[base + t] & hi_mask, jnp.float32) * h_tiles[t]
                p = term if p is None else p + term
            ps.append(p)
        a_parts.append(jnp.sum(_sum_sublanes_of_8(ps), axis=-1, keepdims=True))
    a = jnp.concatenate(a_parts, axis=0)
    act = 0.5 * a * (1.0 + lax.erf(a * (1.0 / math.sqrt(2.0))))
    w_scr[...] = jnp.broadcast_to(gate_col * act, (nsel, LANES))
    outs = [jnp.zeros((SUBLANES, LANES), jnp.float32) for _ in range(tpe)]
    for k in range(nsel):
        wk = jnp.broadcast_to(w_scr[k:k + 1, :], (SUBLANES, LANES))
        base = (j * nsel + k) * tpe
        for t in range(tpe):
            outs[t] = outs[t] + lax.bitcast_convert_type(buf[base + t] << 16, jnp.float32) * wk
    return outs


def _peer_kernel(idx_ref, idx_next_ref, gate_ref, h_ref, x_ref, mod_ref, lng_ref, lnb_ref, uv_hbm, o_ref,
                 *scratch, alpha):
    bufs, y_scr, w_scr, sems = scratch[:NBUF_PEER], scratch[NBUF_PEER], scratch[NBUF_PEER + 1], scratch[NBUF_PEER + 2]
    s = pl.program_id(0)
    tb = TB_PEER
    ahead = NBUF_PEER - 1
    tpe = h_ref.shape[1] // (SUBLANES * LANES)

    @pl.when(s == 0)
    def _():
        for p in range(ahead):
            for j in range(tb):
                _peer_start_token(idx_ref, p * tb + j, uv_hbm, bufs[p], sems.at[p], j)

    gate_t = gate_ref[...].T
    for p in range(NBUF_PEER):
        _peer_wait_block(uv_hbm, bufs[p], sems.at[p])
        q = p + ahead
        dst = q % NBUF_PEER
        src_idx, src_row0 = (idx_ref, q * tb) if q < NBUF_PEER else (idx_next_ref, (q - NBUF_PEER) * tb)
        for j in range(tb):
            _peer_start_token(src_idx, src_row0 + j, uv_hbm, bufs[dst], sems.at[dst], j)
            row = p * tb + j
            h_tiles = [jnp.concatenate([h_ref[row:row + 1, (t * SUBLANES + i) * LANES:(t * SUBLANES + i + 1) * LANES]
                                        for i in range(SUBLANES)], axis=0) for t in range(tpe)]
            outs = _peer_token(bufs[p], j, h_tiles, gate_t[:, row:row + 1], w_scr)
            for t in range(tpe):
                for i in range(SUBLANES):
                    c = t * SUBLANES + i
                    y_scr[row:row + 1, c * LANES:(c + 1) * LANES] = outs[t][i:i + 1, :]

    @pl.when(s == pl.num_programs(0) - 1)
    def _():
        for p in range(ahead):
            _peer_wait_block(uv_hbm, bufs[p], sems.at[p])

    ga2 = mod_ref[0, 5:6, :]
    o_ref[...] = _layer_norm(alpha * x_ref[...] + ga2 * y_scr[...]) * lng_ref[...] + lnb_ref[...]


def _peer_call(idx, gate, h2, x1, mod_l, ln_g, ln_b, uv_tab, *, alpha, mod_idx_fn):
    r, d = h2.shape
    nsel = PEER_HEADS * PEER_TOPK
    rows = NBUF_PEER * TB_PEER
    nstep = r // rows
    nxt = lambda s: jnp.minimum(s + 1, nstep - 1)
    return pl.pallas_call(
        functools.partial(_peer_kernel, alpha=alpha),
        grid=(nstep,),
        in_specs=[
            pl.BlockSpec((rows, nsel), lambda s: (s, 0), memory_space=pltpu.SMEM),
            pl.BlockSpec((rows, nsel), lambda s: (nxt(s), 0), memory_space=pltpu.SMEM),
            pl.BlockSpec((rows, nsel), lambda s: (s, 0)),
            pl.BlockSpec((rows, d), lambda s: (s, 0)),
            pl.BlockSpec((rows, d), lambda s: (s, 0)),
            pl.BlockSpec((1, 6, d), lambda s: (mod_idx_fn(s * rows), 0, 0)),
            pl.BlockSpec((1, d), lambda s: (0, 0)),
            pl.BlockSpec((1, d), lambda s: (0, 0)),
            pl.BlockSpec(memory_space=pl.ANY),
        ],
        out_specs=pl.BlockSpec((rows, d), lambda s: (s, 0)),
        out_shape=jax.ShapeDtypeStruct((r, d), jnp.float32),
        scratch_shapes=[
            *[pltpu.VMEM((TB_PEER * nsel * (d // (SUBLANES * LANES)), SUBLANES, LANES), jnp.uint32)
              for _ in range(NBUF_PEER)],
            pltpu.VMEM((rows, d), jnp.float32),
            pltpu.VMEM((nsel, LANES), jnp.float32),
            pltpu.SemaphoreType.DMA((NBUF_PEER,)),
        ],
        compiler_params=pltpu.CompilerParams(
            dimension_semantics=("arbitrary",), vmem_limit_bytes=VMEM_LIMIT_BYTES),
        name="peer_experts",
    )(idx, idx, gate, h2, x1, mod_l, ln_g.reshape(1, d), ln_b.reshape(1, d), uv_tab)


def _rope_tables(batch, seq, n_ctx_rows):
    rows = seq // GRID_W
    row = jnp.broadcast_to(jnp.arange(rows)[:, None], (rows, GRID_W)).reshape(-1)
    col = jnp.broadcast_to(jnp.arange(GRID_W)[None, :], (rows, GRID_W)).reshape(-1)
    inv = ROPE_BASE ** (-jnp.arange(0, ROT_HALF, 2, dtype=jnp.float32) / ROT_HALF)
    ar = row.astype(jnp.float32)[:, None] * inv
    ac = col.astype(jnp.float32)[:, None] * inv
    ang = jnp.concatenate([ar, ar, ac, ac], axis=-1)
    cos, sin = jnp.cos(ang), jnp.sin(ang)
    quarter = jnp.arange(HEAD_DIM) // (HEAD_DIM // 4)
    sign = jnp.where((quarter == 0) | (quarter == 2), -1.0, 1.0).astype(jnp.float32)
    sin_signed = sin * sign
    cos_t = jnp.concatenate([jnp.tile(cos, (batch, 1)), jnp.ones((n_ctx_rows, HEAD_DIM), jnp.float32)], axis=0)
    sin_t = jnp.concatenate([jnp.tile(sin_signed, (batch, 1)), jnp.zeros((n_ctx_rows, HEAD_DIM), jnp.float32)], axis=0)
    return cos_t, sin_t


def kernel(x, c, ctx, c_ctx, w_ada, b_ada, w_in, conv_w, attn_sink, g_attn, g_conv, w_out, ln1_g, ln1_b,
           peer_wq, peer_keys, peer_u, peer_v, ln2_g, ln2_b):
    batch, seq, d = x.shape
    ctx_len = ctx.shape[1]
    depth = w_ada.shape[0]
    conv_width = conv_w.shape[2]
    attn_w = N_Q_HEADS * HEAD_DIM
    kv_w = N_KV_HEADS * HEAD_DIM
    off_q = 3 * conv_width
    off_k = off_q + attn_w
    off_v = off_k + kv_w
    n_lat = batch * seq
    r = n_lat + batch * ctx_len
    alpha = (2.0 * depth) ** 0.25
    assert batch + 1 <= SUBLANES and seq % TM_IN == 0 and (batch * ctx_len) % TM_IN == 0
    assert ctx_len == TM_OUT and seq % TM_OUT == 0 and ctx_len % BLOCK == 0 and seq % GRID_W == 0
    assert r % (NBUF_PEER * TB_PEER) == 0 and seq % (NBUF_PEER * TB_PEER) == 0 and r % TK_TOPK == 0
    assert w_in.shape[2] % TN_IN == 0 and w_ada.shape[2] % TN_ADA == 0 and r % TM_PQ == 0

    def mod_idx_fn(row0):
        return jnp.where(row0 < n_lat, row0 // seq, batch)

    def first_fn(row0):
        return jnp.where(row0 < n_lat, row0 % seq == 0, (row0 - n_lat) % ctx_len == 0)

    def last_fn(row0):
        end = row0 + TM_OUT
        return jnp.where(row0 < n_lat, end % seq == 0, (end - n_lat) % ctx_len == 0)

    x_all = jnp.concatenate([x.reshape(n_lat, d), ctx.reshape(batch * ctx_len, d)], axis=0)
    cond_rows = jnp.concatenate([c, c_ctx[None, :], jnp.zeros((SUBLANES - batch - 1, d), jnp.float32)], axis=0)
    mod = _ada_call(cond_rows.T, w_ada, b_ada, batch + 1).reshape(depth, SUBLANES, 6, d)
    cos_t, sin_t = _rope_tables(batch, seq, batch * ctx_len)
    u_bits = lax.bitcast_convert_type(peer_u.astype(jnp.bfloat16), jnp.uint16).astype(jnp.uint32)
    v_bits = lax.bitcast_convert_type(peer_v.astype(jnp.bfloat16), jnp.uint16).astype(jnp.uint32)
    n_experts = peer_u.shape[1]
    tpe = d // (SUBLANES * LANES)
    uv_tab = ((u_bits << 16) | v_bits).reshape(depth * n_experts * tpe, SUBLANES, LANES)

    for layer in range(depth):
        mod_l = mod[layer]
        p = _in_proj_call(x_all, mod_l, w_in[layer].astype(jnp.bfloat16), mod_idx_fn)
        ya = _attn_call(p, attn_sink[layer], g_attn[layer], cos_t, sin_t, batch=batch, seq=seq, ctx_len=ctx_len,
                        off_q=off_q, off_k=off_k, off_v=off_v)
        x1, h2 = _out_call(ya, p, conv_w[layer], g_conv[layer], w_out[layer].astype(jnp.bfloat16), x_all, mod_l,
                           ln1_g[layer], ln1_b[layer], alpha=alpha, mod_idx_fn=mod_idx_fn, first_fn=first_fn,
                           last_fn=last_fn, conv_w=conv_width)
        keys_bf = peer_keys[layer].reshape(2 * PEER_HEADS, N_KEYS, PEER_HALF).astype(jnp.bfloat16)
        s = _peer_score_call(h2, peer_wq[layer].astype(jnp.bfloat16), keys_bf)
        idx, gate = _topk_call(s, rows_per_expert=tpe, row_base=layer * n_experts * tpe)
        x_all = _peer_call(idx, gate, h2, x1, mod_l, ln2_g[layer], ln2_b[layer], uv_tab,
                           alpha=alpha, mod_idx_fn=mod_idx_fn)
    return x_all[:n_lat].reshape(batch, seq, d)
```

```python
import functools
import math

import jax
import jax.numpy as jnp
from jax import lax
from jax.experimental import pallas as pl
from jax.experimental.pallas import tpu as pltpu

HEAD_DIM = 128
N_Q_HEADS = 8
N_KV_HEADS = 2
Q_PER_KV = N_Q_HEADS // N_KV_HEADS
GRID_W = 64
WINDOW = 128
BLOCK = 128
ROPE_BASE = 10000.0
ROT_HALF = HEAD_DIM // 2
CONV_K = 3
PEER_HEADS = 8
N_KEYS = 128
PEER_TOPK = 16
PEER_HALF = 128
LN_EPS = 1e-6
NEG_BIG = -1e30

LANES = 128
SUBLANES = 8
VMEM_LIMIT_BYTES = 56 * 1024 * 1024

TM_IN = 512
TN_IN = 1536
TM_OUT = 256
TM_PQ = 256
TK_TOPK = 128
TB_PEER = 8
NBUF_PEER = 4
TN_ADA = 512


def _layer_norm(x):
    mu = jnp.mean(x, axis=-1, keepdims=True)
    xc = x - mu
    var = jnp.mean(xc * xc, axis=-1, keepdims=True)
    return xc * lax.rsqrt(var + LN_EPS)


def _ada_kernel(condt_ref, w_ref, b_ref, o_ref, *, n_rows):
    ct = condt_ref[...]
    ct = ct * (1.0 / (1.0 + jnp.exp(-ct)))
    w = w_ref[0]
    rows = []
    for r in range(SUBLANES):
        if r < n_rows:
            rows.append(jnp.sum(w * ct[:, r:r + 1], axis=0, keepdims=True) + b_ref[0])
        else:
            rows.append(jnp.zeros_like(b_ref[0]))
    o_ref[0] = jnp.concatenate(rows, axis=0)


def _ada_call(condt, w_ada, b_ada, n_rows):
    depth, d, n6 = w_ada.shape
    return pl.pallas_call(
        functools.partial(_ada_kernel, n_rows=n_rows),
        grid=(depth, n6 // TN_ADA),
        in_specs=[
            pl.BlockSpec((d, SUBLANES), lambda l, j: (0, 0)),
            pl.BlockSpec((1, d, TN_ADA), lambda l, j: (l, 0, j)),
            pl.BlockSpec((1, 1, TN_ADA), lambda l, j: (l, 0, j)),
        ],
        out_specs=pl.BlockSpec((1, SUBLANES, TN_ADA), lambda l, j: (l, 0, j)),
        out_shape=jax.ShapeDtypeStruct((depth, SUBLANES, n6), jnp.float32),
        compiler_params=pltpu.CompilerParams(
            dimension_semantics=("arbitrary", "arbitrary"), vmem_limit_bytes=VMEM_LIMIT_BYTES),
        name="ada_mod",
    )(condt, w_ada, b_ada.reshape(depth, 1, n6))


def _in_proj_kernel(x_ref, mod_ref, w_ref, o_ref, h_scr):
    @pl.when(pl.program_id(1) == 0)
    def _():
        sh = mod_ref[0, 0:1, :]
        sc = mod_ref[0, 1:2, :]
        h_scr[...] = (_layer_norm(x_ref[...]) * (1.0 + sc) + sh).astype(jnp.bfloat16)

    o_ref[...] = jnp.dot(h_scr[...], w_ref[...], preferred_element_type=jnp.float32)


def _in_proj_call(x_all, mod_l, w_in_bf, mod_idx_fn):
    r, d = x_all.shape
    n = w_in_bf.shape[1]
    return pl.pallas_call(
        _in_proj_kernel,
        grid=(r // TM_IN, n // TN_IN),
        in_specs=[
            pl.BlockSpec((TM_IN, d), lambda i, j: (i, 0)),
            pl.BlockSpec((1, 6, d), lambda i, j: (mod_idx_fn(i * TM_IN), 0, 0)),
            pl.BlockSpec((d, TN_IN), lambda i, j: (0, j)),
        ],
        out_specs=pl.BlockSpec((TM_IN, TN_IN), lambda i, j: (i, j)),
        out_shape=jax.ShapeDtypeStruct((r, n), jnp.float32),
        scratch_shapes=[pltpu.VMEM((TM_IN, d), jnp.bfloat16)],
        compiler_params=pltpu.CompilerParams(
            dimension_semantics=("arbitrary", "arbitrary"), vmem_limit_bytes=VMEM_LIMIT_BYTES),
        name="in_proj",
    )(x_all, mod_l, w_in_bf)


def _rope(x, cos, sin_signed, first_or_third):
    rot = jnp.where(first_or_third, pltpu.roll(x, 3 * HEAD_DIM // 4, axis=1), pltpu.roll(x, HEAD_DIM // 4, axis=1))
    return x * cos + rot * sin_signed


def _attn_kernel(sink_ref, q_ref, kp_ref, k0_ref, kn_ref, vp_ref, v0_ref, vn_ref, kc_ref, vc_ref,
                 cq_ref, sq_ref, cp_ref, sp_ref, cn_ref, sn_ref, g_ref, o_ref, *, nb_lat, nb_seq):
    t = pl.program_id(0)
    is_lat = t < nb_lat
    n = t % nb_seq
    has_prev = jnp.logical_and(is_lat, n > 0)
    has_next = jnp.logical_and(is_lat, n < nb_seq - 1)
    scale = HEAD_DIM ** -0.5

    lane = lax.broadcasted_iota(jnp.int32, (BLOCK, HEAD_DIM), 1)
    quarter = lane // (HEAD_DIM // 4)
    fot = jnp.logical_or(quarter == 0, quarter == 2)

    cq, sq = cq_ref[...], sq_ref[...]
    cp, sp = cp_ref[...], sp_ref[...]
    cn, sn = cn_ref[...], sn_ref[...]

    qi = lax.broadcasted_iota(jnp.int32, (BLOCK, 3 * BLOCK), 0)
    kj = lax.broadcasted_iota(jnp.int32, (BLOCK, 3 * BLOCK), 1)
    blk = kj // BLOCK
    rel = kj - BLOCK - qi
    lo_blk = jnp.where(is_lat, jnp.where(has_prev, 0, 1), 3)
    hi_blk = jnp.where(has_next, 2, 1)
    valid = (jnp.abs(rel) <= WINDOW) & (blk >= lo_blk) & (blk <= hi_blk)
    bias = jnp.where(valid, 0.0, NEG_BIG)
    bias4 = jnp.concatenate([bias] * Q_PER_KV, axis=0)

    outs = []
    for g in range(N_KV_HEADS):
        ksl = slice(g * HEAD_DIM, (g + 1) * HEAD_DIM)
        kw = jnp.concatenate([
            _rope(kp_ref[:, ksl], cp, sp, fot),
            _rope(k0_ref[:, ksl], cq, sq, fot),
            _rope(kn_ref[:, ksl], cn, sn, fot)], axis=0).astype(jnp.bfloat16)
        vw = jnp.concatenate([vp_ref[:, ksl], v0_ref[:, ksl], vn_ref[:, ksl]], axis=0).astype(jnp.bfloat16)
        kc = kc_ref[:, ksl].astype(jnp.bfloat16)
        vc = vc_ref[:, ksl].astype(jnp.bfloat16)
        qs = []
        sinks = []
        for r in range(Q_PER_KV):
            h = g * Q_PER_KV + r
            qs.append(_rope(q_ref[:, h * HEAD_DIM:(h + 1) * HEAD_DIM], cq, sq, fot))
            sinks.append(jnp.full((BLOCK, 1), sink_ref[h], jnp.float32))
        qg = jnp.concatenate(qs, axis=0).astype(jnp.bfloat16)
        sink = jnp.concatenate(sinks, axis=0)
        s_w = lax.dot_general(qg, kw, (((1,), (1,)), ((), ())), preferred_element_type=jnp.float32) * scale
        s_c = lax.dot_general(qg, kc, (((1,), (1,)), ((), ())), preferred_element_type=jnp.float32) * scale
        s_w = jnp.where(bias4 == 0.0, s_w, NEG_BIG)
        m = jnp.maximum(jnp.maximum(jnp.max(s_w, axis=-1, keepdims=True), jnp.max(s_c, axis=-1, keepdims=True)), sink)
        p_w = jnp.exp(s_w - m)
        p_c = jnp.exp(s_c - m)
        denom = jnp.sum(p_w, axis=-1, keepdims=True) + jnp.sum(p_c, axis=-1, keepdims=True) + jnp.exp(sink - m)
        o = (jnp.dot(p_w.astype(jnp.bfloat16), vw, preferred_element_type=jnp.float32)
             + jnp.dot(p_c.astype(jnp.bfloat16), vc, preferred_element_type=jnp.float32)) / denom
        o = o * lax.rsqrt(jnp.mean(o * o, axis=-1, keepdims=True) + LN_EPS)
        for r in range(Q_PER_KV):
            h = g * Q_PER_KV + r
            outs.append(o[r * BLOCK:(r + 1) * BLOCK, :] * g_ref[:, h * HEAD_DIM:(h + 1) * HEAD_DIM])
    o_ref[...] = jnp.concatenate(outs, axis=1)


def _attn_call(p, sink_l, g_attn_l, cos_t, sin_t, *, batch, seq, ctx_len, off_q, off_k, off_v):
    r = p.shape[0]
    nb_seq = seq // BLOCK
    nb_lat = batch * nb_seq
    nb_ctx = ctx_len // BLOCK
    attn_w = N_Q_HEADS * HEAD_DIM
    kv_w = N_KV_HEADS * HEAD_DIM
    qb, kb, vb = off_q // attn_w, off_k // kv_w, off_v // kv_w
    last = r // BLOCK - 1

    def prev_i(t):
        return jnp.maximum(t - 1, 0)

    def next_i(t):
        return jnp.minimum(t + 1, last)

    def ctx_i(t):
        b = jnp.where(t < nb_lat, t // nb_seq, (t - nb_lat) // nb_ctx)
        return (batch * seq) // ctx_len + b

    kv_spec = lambda fn, col: pl.BlockSpec((BLOCK, kv_w), lambda t: (fn(t), col))
    cs_spec = lambda fn: pl.BlockSpec((BLOCK, HEAD_DIM), lambda t: (fn(t), 0))
    ident = lambda t: t
    return pl.pallas_call(
        functools.partial(_attn_kernel, nb_lat=nb_lat, nb_seq=nb_seq),
        grid=(r // BLOCK,),
        in_specs=[
            pl.BlockSpec(memory_space=pltpu.SMEM),
            pl.BlockSpec((BLOCK, attn_w), lambda t: (t, qb)),
            kv_spec(prev_i, kb), kv_spec(ident, kb), kv_spec(next_i, kb),
            kv_spec(prev_i, vb), kv_spec(ident, vb), kv_spec(next_i, vb),
            pl.BlockSpec((ctx_len, kv_w), lambda t: (ctx_i(t), kb)),
            pl.BlockSpec((ctx_len, kv_w), lambda t: (ctx_i(t), vb)),
            cs_spec(ident), cs_spec(ident), cs_spec(prev_i), cs_spec(prev_i), cs_spec(next_i), cs_spec(next_i),
            pl.BlockSpec((1, attn_w), lambda t: (0, 0)),
        ],
        out_specs=pl.BlockSpec((BLOCK, attn_w), lambda t: (t, 0)),
        out_shape=jax.ShapeDtypeStruct((r, attn_w), jnp.float32),
        compiler_params=pltpu.CompilerParams(
            dimension_semantics=("arbitrary",), vmem_limit_bytes=VMEM_LIMIT_BYTES),
        name="attn",
    )(sink_l, p, p, p, p, p, p, p, p, p, cos_t, sin_t, cos_t, sin_t, cos_t, sin_t, g_attn_l.reshape(1, attn_w))


def _out_kernel(ya_ref, gb_ref, gc_ref, xt_ref, gcp_ref, xtp_ref, gcn_ref, xtn_ref, cw_ref, gconv_ref,
                w_ref, x_ref, mod_ref, lng_ref, lnb_ref, x1_ref, h2_ref, *, alpha, first_fn, last_fn, conv_w):
    i = pl.program_id(0)
    r0 = i * TM_OUT
    is_first = first_fn(r0)
    is_last = last_fn(r0)
    u = gc_ref[...] * xt_ref[...]
    u_prev_row = jnp.where(is_first, 0.0, gcp_ref[SUBLANES - 1:SUBLANES, :] * xtp_ref[SUBLANES - 1:SUBLANES, :])
    u_next_row = jnp.where(is_last, 0.0, gcn_ref[0:1, :] * xtn_ref[0:1, :])
    row = lax.broadcasted_iota(jnp.int32, u.shape, 0)
    up = jnp.where(row == 0, u_prev_row, pltpu.roll(u, 1, axis=0))
    un = jnp.where(row == TM_OUT - 1, u_next_row, pltpu.roll(u, TM_OUT - 1, axis=0))
    yc = gb_ref[...] * (up * cw_ref[0:1, :] + u * cw_ref[1:2, :] + un * cw_ref[2:3, :])
    parts = []
    for g in range(conv_w // HEAD_DIM):
        blk = yc[:, g * HEAD_DIM:(g + 1) * HEAD_DIM]
        parts.append(blk * lax.rsqrt(jnp.mean(blk * blk, axis=-1, keepdims=True) + LN_EPS))
    ycn = jnp.concatenate(parts, axis=1) * gconv_ref[...]
    cat = jnp.concatenate([ya_ref[...], ycn], axis=1).astype(jnp.bfloat16)
    y = jnp.dot(cat, w_ref[...], preferred_element_type=jnp.float32)
    ga1 = mod_ref[0, 2:3, :]
    sh2 = mod_ref[0, 3:4, :]
    sc2 = mod_ref[0, 4:5, :]
    x1 = _layer_norm(alpha * x_ref[...] + ga1 * y) * lng_ref[...] + lnb_ref[...]
    x1_ref[...] = x1
    h2_ref[...] = _layer_norm(x1) * (1.0 + sc2) + sh2


def _out_call(ya, p, conv_w_l, g_conv_l, w_out_bf, x_all, mod_l, ln_g, ln_b, *, alpha, mod_idx_fn, first_fn, last_fn,
              conv_w):
    r, d = x_all.shape
    nblk8 = r // SUBLANES
    hb = TM_OUT // SUBLANES

    def prev8(i):
        return jnp.maximum(i * hb - 1, 0)

    def next8(i):
        return jnp.minimum((i + 1) * hb, nblk8 - 1)

    tile = lambda col: pl.BlockSpec((TM_OUT, conv_w), lambda i: (i, col))
    halo = lambda fn, col: pl.BlockSpec((SUBLANES, conv_w), lambda i: (fn(i), col))
    vec = lambda w: pl.BlockSpec((1, w), lambda i: (0, 0))
    return pl.pallas_call(
        functools.partial(_out_kernel, alpha=alpha, first_fn=first_fn, last_fn=last_fn, conv_w=conv_w),
        grid=(r // TM_OUT,),
        in_specs=[
            pl.BlockSpec((TM_OUT, ya.shape[1]), lambda i: (i, 0)),
            tile(0), tile(1), tile(2),
            halo(prev8, 1), halo(prev8, 2), halo(next8, 1), halo(next8, 2),
            pl.BlockSpec((CONV_K, conv_w), lambda i: (0, 0)),
            vec(conv_w),
            pl.BlockSpec(w_out_bf.shape, lambda i: (0, 0)),
            pl.BlockSpec((TM_OUT, d), lambda i: (i, 0)),
            pl.BlockSpec((1, 6, d), lambda i: (mod_idx_fn(i * TM_OUT), 0, 0)),
            vec(d), vec(d),
        ],
        out_specs=[pl.BlockSpec((TM_OUT, d), lambda i: (i, 0)), pl.BlockSpec((TM_OUT, d), lambda i: (i, 0))],
        out_shape=[jax.ShapeDtypeStruct((r, d), jnp.float32), jax.ShapeDtypeStruct((r, d), jnp.float32)],
        compiler_params=pltpu.CompilerParams(
            dimension_semantics=("arbitrary",), vmem_limit_bytes=VMEM_LIMIT_BYTES),
        name="conv_out_proj",
    )(ya, p, p, p, p, p, p, p, conv_w_l, g_conv_l.reshape(1, conv_w), w_out_bf, x_all, mod_l,
      ln_g.reshape(1, d), ln_b.reshape(1, d))


def _peer_score_kernel(h_ref, wq_ref, keys_ref, s_ref):
    q = jnp.dot(h_ref[...].astype(jnp.bfloat16), wq_ref[...], preferred_element_type=jnp.float32)
    qb = q.astype(jnp.bfloat16)
    for hp in range(2 * PEER_HEADS):
        s_ref[hp] = lax.dot_general(keys_ref[hp], qb[:, hp * PEER_HALF:(hp + 1) * PEER_HALF],
                                    (((1,), (1,)), ((), ())), preferred_element_type=jnp.float32)


def _peer_score_call(h2, wq_bf, keys_bf):
    r, d = h2.shape
    nq = wq_bf.shape[1]
    return pl.pallas_call(
        _peer_score_kernel,
        grid=(r // TM_PQ,),
        in_specs=[
            pl.BlockSpec((TM_PQ, d), lambda i: (i, 0)),
            pl.BlockSpec((d, nq), lambda i: (0, 0)),
            pl.BlockSpec(keys_bf.shape, lambda i: (0, 0, 0)),
        ],
        out_specs=pl.BlockSpec((2 * PEER_HEADS, N_KEYS, TM_PQ), lambda i: (0, 0, i)),
        out_shape=jax.ShapeDtypeStruct((2 * PEER_HEADS, N_KEYS, r), jnp.float32),
        compiler_params=pltpu.CompilerParams(
            dimension_semantics=("arbitrary",), vmem_limit_bytes=VMEM_LIMIT_BYTES),
        name="peer_scores",
    )(h2, wq_bf, keys_bf)


def _select_max(x, row, sentinel):
    m = jnp.max(x, axis=0, keepdims=True)
    pos = jnp.min(jnp.where(x == m, row, sentinel), axis=0, keepdims=True)
    return m, pos, jnp.where(row == pos, -jnp.inf, x)


def _topk_kernel(s_ref, idx_ref, gate_ref, *, rows_per_expert, row_base):
    tt = s_ref.shape[2]
    k = PEER_TOPK
    key_row = lax.broadcasted_iota(jnp.int32, (N_KEYS, tt), 0).astype(jnp.float32)
    k_row = lax.broadcasted_iota(jnp.int32, (k, tt), 0)
    row8_i = lax.broadcasted_iota(jnp.int32, (SUBLANES, tt), 0)
    row8 = row8_i.astype(jnp.float32)
    flat_row = jnp.concatenate([row8, row8 + SUBLANES] + [a * k + row8 for a in range(1, SUBLANES)]
                               + [(SUBLANES + row8) * k], axis=0)
    gates, idxs = [], []
    for h in range(PEER_HEADS):
        x1 = s_ref[2 * h]
        x2 = s_ref[2 * h + 1]
        m1s, p1s = [], []
        s1 = jnp.zeros((k, tt), jnp.float32)
        i1 = jnp.zeros((k, tt), jnp.float32)
        s2 = jnp.zeros((k, tt), jnp.float32)
        i2 = jnp.zeros((k, tt), jnp.float32)
        for a in range(k):
            m1, p1, x1 = _select_max(x1, key_row, float(N_KEYS))
            m2, p2, x2 = _select_max(x2, key_row, float(N_KEYS))
            m1s.append(m1)
            p1s.append(p1)
            s1 = jnp.where(k_row == a, m1, s1)
            i1 = jnp.where(k_row == a, p1, i1)
            s2 = jnp.where(k_row == a, m2, s2)
            i2 = jnp.where(k_row == a, p2, i2)
        s2_lo, i2_lo = s2[0:SUBLANES], i2[0:SUBLANES]
        vals = [m1s[0] + s2_lo, m1s[0] + s2[SUBLANES:k]]
        ids = [p1s[0] * N_KEYS + i2_lo, p1s[0] * N_KEYS + i2[SUBLANES:k]]
        for a in range(1, SUBLANES):
            vals.append(jnp.where(row8_i < k // (a + 1), m1s[a] + s2_lo, -jnp.inf))
            ids.append(p1s[a] * N_KEYS + i2_lo)
        vals.append(s1[SUBLANES:k] + s2[0:1])
        ids.append(i1[SUBLANES:k] * N_KEYS + i2[0:1])
        cand = jnp.concatenate(vals, axis=0)
        cand_i = jnp.concatenate(ids, axis=0)
        top_s = jnp.zeros((k, tt), jnp.float32)
        top_i = jnp.zeros((k, tt), jnp.float32)
        m0 = None
        for it in range(k):
            m, pos, cand_next = _select_max(cand, flat_row, float(k * k))
            e = jnp.sum(jnp.where(flat_row == pos, cand_i, 0.0), axis=0, keepdims=True)
            cand = cand_next
            top_s = jnp.where(k_row == it, m, top_s)
            top_i = jnp.where(k_row == it, e, top_i)
            if it == 0:
                m0 = m
        ex = jnp.exp(top_s - m0)
        gates.append(ex / jnp.sum(ex, axis=0, keepdims=True))
        idxs.append(top_i)
    gate_ref[...] = jnp.concatenate(gates, axis=0).T
    idx_ref[...] = (jnp.concatenate(idxs, axis=0).astype(jnp.int32) * rows_per_expert + row_base).T


def _topk_call(s, rows_per_expert=1, row_base=0):
    r = s.shape[2]
    w = PEER_HEADS * PEER_TOPK
    return pl.pallas_call(
        functools.partial(_topk_kernel, rows_per_expert=rows_per_expert, row_base=row_base),
        grid=(r // TK_TOPK,),
        in_specs=[pl.BlockSpec((2 * PEER_HEADS, N_KEYS, TK_TOPK), lambda i: (0, 0, i))],
        out_specs=[pl.BlockSpec((TK_TOPK, w), lambda i: (i, 0)), pl.BlockSpec((TK_TOPK, w), lambda i: (i, 0))],
        out_shape=[jax.ShapeDtypeStruct((r, w), jnp.int32), jax.ShapeDtypeStruct((r, w), jnp.float32)],
        compiler_params=pltpu.CompilerParams(
            dimension_semantics=("arbitrary",), vmem_limit_bytes=VMEM_LIMIT_BYTES),
        name="peer_topk",
    )(s)


def _peer_start_token(idx_ref, row, uv_hbm, buf, sem, j):
    nsel = PEER_HEADS * PEER_TOPK
    tpe = buf.shape[0] // (TB_PEER * nsel)
    for k in range(nsel):
        first = pl.multiple_of(idx_ref[row, k], tpe)
        pltpu.make_async_copy(uv_hbm.at[pl.ds(first, tpe)], buf.at[pl.ds((j * nsel + k) * tpe, tpe)],
                              sem).start(priority=k % 2)


def _peer_wait_block(uv_hbm, buf, sem):
    pltpu.make_async_copy(uv_hbm.at[pl.ds(0, buf.shape[0])], buf, sem).wait()


def _sum_sublanes_of_8(ps):
    sub = lax.broadcasted_iota(jnp.int32, (SUBLANES, LANES), 0)
    dist = SUBLANES // 2
    while len(ps) > 1:
        half = len(ps) // 2
        low = (sub % (2 * dist)) < dist
        ps = [jnp.where(low, ps[i], ps[i + half])
              + jnp.where(low, pltpu.roll(ps[i], SUBLANES - dist, axis=0), pltpu.roll(ps[i + half], dist, axis=0))
              for i in range(half)]
        dist //= 2
    return ps[0]


def _peer_token(buf, j, h_tiles, gate_col, w_scr):
    nsel = PEER_HEADS * PEER_TOPK
    tpe = len(h_tiles)
    hi_mask = jnp.uint32(0xFFFF0000)
    a_parts = []
    for g in range(nsel // SUBLANES):
        ps = []
        for i in range(SUBLANES):
            base = (j * nsel + g * SUBLANES + i) * tpe
            p = None
            for t in range(tpe):
                term = lax.bitcast_convert_type(buf[base + t] & hi_mask, jnp.float32) * h_tiles[t]
                p = term if p is None else p + term
            ps.append(p)
        a_parts.append(jnp.sum(_sum_sublanes_of_8(ps), axis=-1, keepdims=True))
    a = jnp.concatenate(a_parts, axis=0)
    act = 0.5 * a * (1.0 + lax.erf(a * (1.0 / math.sqrt(2.0))))
    w_scr[...] = jnp.broadcast_to(gate_col * act, (nsel, LANES))
    outs = [jnp.zeros((SUBLANES, LANES), jnp.float32) for _ in range(tpe)]
    for k in range(nsel):
        wk = jnp.broadcast_to(w_scr[k:k + 1, :], (SUBLANES, LANES))
        base = (j * nsel + k) * tpe
        for t in range(tpe):
            outs[t] = outs[t] + lax.bitcast_convert_type(buf[base + t] << 16, jnp.float32) * wk
    return outs


def _peer_kernel(idx_ref, idx_next_ref, gate_ref, h_ref, x_ref, mod_ref, lng_ref, lnb_ref, uv_hbm, o_ref,
                 *scratch, alpha):
    bufs, y_scr, w_scr, sems = scratch[:NBUF_PEER], scratch[NBUF_PEER], scratch[NBUF_PEER + 1], scratch[NBUF_PEER + 2]
    s = pl.program_id(0)
    tb = TB_PEER
    ahead = NBUF_PEER - 1
    tpe = h_ref.shape[1] // (SUBLANES * LANES)

    @pl.when(s == 0)
    def _():
        for p in range(ahead):
            for j in range(tb):
                _peer_start_token(idx_ref, p * tb + j, uv_hbm, bufs[p], sems.at[p], j)

    gate_t = gate_ref[...].T
    for p in range(NBUF_PEER):
        _peer_wait_block(uv_hbm, bufs[p], sems.at[p])
        q = p + ahead
        dst = q % NBUF_PEER
        src_idx, src_row0 = (idx_ref, q * tb) if q < NBUF_PEER else (idx_next_ref, (q - NBUF_PEER) * tb)
        for j in range(tb):
            _peer_start_token(src_idx, src_row0 + j, uv_hbm, bufs[dst], sems.at[dst], j)
            row = p * tb + j
            h_tiles = [jnp.concatenate([h_ref[row:row + 1, (t * SUBLANES + i) * LANES:(t * SUBLANES + i + 1) * LANES]
                                        for i in range(SUBLANES)], axis=0) for t in range(tpe)]
            outs = _peer_token(bufs[p], j, h_tiles, gate_t[:, row:row + 1], w_scr)
            for t in range(tpe):
                for i in range(SUBLANES):
                    c = t * SUBLANES + i
                    y_scr[row:row + 1, c * LANES:(c + 1) * LANES] = outs[t][i:i + 1, :]

    @pl.when(s == pl.num_programs(0) - 1)
    def _():
        for p in range(ahead):
            _peer_wait_block(uv_hbm, bufs[p], sems.at[p])

    ga2 = mod_ref[0, 5:6, :]
    o_ref[...] = _layer_norm(alpha * x_ref[...] + ga2 * y_scr[...]) * lng_ref[...] + lnb_ref[...]


def _peer_call(idx, gate, h2, x1, mod_l, ln_g, ln_b, uv_tab, *, alpha, mod_idx_fn):
    r, d = h2.shape
    nsel = PEER_HEADS * PEER_TOPK
    rows = NBUF_PEER * TB_PEER
    nstep = r // rows
    nxt = lambda s: jnp.minimum(s + 1, nstep - 1)
    return pl.pallas_call(
        functools.partial(_peer_kernel, alpha=alpha),
        grid=(nstep,),
        in_specs=[
            pl.BlockSpec((rows, nsel), lambda s: (s, 0), memory_space=pltpu.SMEM),
            pl.BlockSpec((rows, nsel), lambda s: (nxt(s), 0), memory_space=pltpu.SMEM),
            pl.BlockSpec((rows, nsel), lambda s: (s, 0)),
            pl.BlockSpec((rows, d), lambda s: (s, 0)),
            pl.BlockSpec((rows, d), lambda s: (s, 0)),
            pl.BlockSpec((1, 6, d), lambda s: (mod_idx_fn(s * rows), 0, 0)),
            pl.BlockSpec((1, d), lambda s: (0, 0)),
            pl.BlockSpec((1, d), lambda s: (0, 0)),
            pl.BlockSpec(memory_space=pl.ANY),
        ],
        out_specs=pl.BlockSpec((rows, d), lambda s: (s, 0)),
        out_shape=jax.ShapeDtypeStruct((r, d), jnp.float32),
        scratch_shapes=[
            *[pltpu.VMEM((TB_PEER * nsel * (d // (SUBLANES * LANES)), SUBLANES, LANES), jnp.uint32)
              for _ in range(NBUF_PEER)],
            pltpu.VMEM((rows, d), jnp.float32),
            pltpu.VMEM((nsel, LANES), jnp.float32),
            pltpu.SemaphoreType.DMA((NBUF_PEER,)),
        ],
        compiler_params=pltpu.CompilerParams(
            dimension_semantics=("arbitrary",), vmem_limit_bytes=VMEM_LIMIT_BYTES),
        name="peer_experts",
    )(idx, idx, gate, h2, x1, mod_l, ln_g.reshape(1, d), ln_b.reshape(1, d), uv_tab)


def _rope_tables(batch, seq, n_ctx_rows):
    rows = seq // GRID_W
    row = jnp.broadcast_to(jnp.arange(rows)[:, None], (rows, GRID_W)).reshape(-1)
    col = jnp.broadcast_to(jnp.arange(GRID_W)[None, :], (rows, GRID_W)).reshape(-1)
    inv = ROPE_BASE ** (-jnp.arange(0, ROT_HALF, 2, dtype=jnp.float32) / ROT_HALF)
    ar = row.astype(jnp.float32)[:, None] * inv
    ac = col.astype(jnp.float32)[:, None] * inv
    ang = jnp.concatenate([ar, ar, ac, ac], axis=-1)
    cos, sin = jnp.cos(ang), jnp.sin(ang)
    quarter = jnp.arange(HEAD_DIM) // (HEAD_DIM // 4)
    sign = jnp.where((quarter == 0) | (quarter == 2), -1.0, 1.0).astype(jnp.float32)
    sin_signed = sin * sign
    cos_t = jnp.concatenate([jnp.tile(cos, (batch, 1)), jnp.ones((n_ctx_rows, HEAD_DIM), jnp.float32)], axis=0)
    sin_t = jnp.concatenate([jnp.tile(sin_signed, (batch, 1)), jnp.zeros((n_ctx_rows, HEAD_DIM), jnp.float32)], axis=0)
    return cos_t, sin_t


def kernel(x, c, ctx, c_ctx, w_ada, b_ada, w_in, conv_w, attn_sink, g_attn, g_conv, w_out, ln1_g, ln1_b,
           peer_wq, peer_keys, peer_u, peer_v, ln2_g, ln2_b):
    batch, seq, d = x.shape
    ctx_len = ctx.shape[1]
    depth = w_ada.shape[0]
    conv_width = conv_w.shape[2]
    attn_w = N_Q_HEADS * HEAD_DIM
    kv_w = N_KV_HEADS * HEAD_DIM
    off_q = 3 * conv_width
    off_k = off_q + attn_w
    off_v = off_k + kv_w
    n_lat = batch * seq
    r = n_lat + batch * ctx_len
    alpha = (2.0 * depth) ** 0.25
    assert batch + 1 <= SUBLANES and seq % TM_IN == 0 and (batch * ctx_len) % TM_IN == 0
    assert ctx_len == TM_OUT and seq % TM_OUT == 0 and ctx_len % BLOCK == 0 and seq % GRID_W == 0
    assert r % (NBUF_PEER * TB_PEER) == 0 and seq % (NBUF_PEER * TB_PEER) == 0 and r % TK_TOPK == 0
    assert w_in.shape[2] % TN_IN == 0 and w_ada.shape[2] % TN_ADA == 0 and r % TM_PQ == 0

    def mod_idx_fn(row0):
        return jnp.where(row0 < n_lat, row0 // seq, batch)

    def first_fn(row0):
        return jnp.where(row0 < n_lat, row0 % seq == 0, (row0 - n_lat) % ctx_len == 0)

    def last_fn(row0):
        end = row0 + TM_OUT
        return jnp.where(row0 < n_lat, end % seq == 0, (end - n_lat) % ctx_len == 0)

    x_all = jnp.concatenate([x.reshape(n_lat, d), ctx.reshape(batch * ctx_len, d)], axis=0)
    cond_rows = jnp.concatenate([c, c_ctx[None, :], jnp.zeros((SUBLANES - batch - 1, d), jnp.float32)], axis=0)
    mod = _ada_call(cond_rows.T, w_ada, b_ada, batch + 1).reshape(depth, SUBLANES, 6, d)
    cos_t, sin_t = _rope_tables(batch, seq, batch * ctx_len)
    u_bits = lax.bitcast_convert_type(peer_u.astype(jnp.bfloat16), jnp.uint16).astype(jnp.uint32)
    v_bits = lax.bitcast_convert_type(peer_v.astype(jnp.bfloat16), jnp.uint16).astype(jnp.uint32)
    n_experts = peer_u.shape[1]
    tpe = d // (SUBLANES * LANES)
    uv_tab = ((u_bits << 16) | v_bits).reshape(depth * n_experts * tpe, SUBLANES, LANES)

    for layer in range(depth):
        mod_l = mod[layer]
        p = _in_proj_call(x_all, mod_l, w_in[layer].astype(jnp.bfloat16), mod_idx_fn)
        ya = _attn_call(p, attn_sink[layer], g_attn[layer], cos_t, sin_t, batch=batch, seq=seq, ctx_len=ctx_len,
                        off_q=off_q, off_k=off_k, off_v=off_v)
        x1, h2 = _out_call(ya, p, conv_w[layer], g_conv[layer], w_out[layer].astype(jnp.bfloat16), x_all, mod_l,
                           ln1_g[layer], ln1_b[layer], alpha=alpha, mod_idx_fn=mod_idx_fn, first_fn=first_fn,
                           last_fn=last_fn, conv_w=conv_width)
        keys_bf = peer_keys[layer].reshape(2 * PEER_HEADS, N_KEYS, PEER_HALF).astype(jnp.bfloat16)
        s = _peer_score_call(h2, peer_wq[layer].astype(jnp.bfloat16), keys_bf)
        idx, gate = _topk_call(s, rows_per_expert=tpe, row_base=layer * n_experts * tpe)
        x_all = _peer_call(idx, gate, h2, x1, mod_l, ln2_g[layer], ln2_b[layer], uv_tab,
                           alpha=alpha, mod_idx_fn=mod_idx_fn)
    return x_all[:n_lat].reshape(batch, seq, d)
```
